```python
import math
import jax, jax.numpy as jnp
from jax import lax
import numpy as np

D_MODEL = 1024
BATCH = 16
SEQ = 2048
DEPTH = 2

MEM_LEN = 256
HEAD_DIM = 64
GMLP_GROUPS = 6
MOBA_HEADS = 6
MEM_HEADS = 4
D_GMLP = GMLP_GROUPS * HEAD_DIM
D_MOBA = MOBA_HEADS * HEAD_DIM
D_MEM = MEM_HEADS * HEAD_DIM
D_MIX = D_GMLP + D_MOBA + D_MEM
D_IN = 2 * D_GMLP + 3 * D_MOBA + D_MEM + D_MIX
GMLP_CHUNK = 128
MOBA_BLOCK = 256
MOBA_TOPK = 3
MOBA_Q_CHUNK = 32
REL_BUCKETS = 32
REL_MAX_DIST = 128
EPS = 1e-6
NEG = -1e30

kernel_name = "hybrid_gmlp_moba_memory_block"


def rmsnorm(x, g):
    xf = x.astype(jnp.float32)
    y = xf * lax.rsqrt(jnp.mean(xf * xf, axis=-1, keepdims=True) + EPS)
    return (y * g.astype(jnp.float32)).astype(x.dtype)


def t5_bucket(dist):
    dist = jnp.maximum(dist, 0)
    max_exact = REL_BUCKETS // 2
    df = jnp.maximum(dist, 1).astype(jnp.float32)
    large = max_exact + (jnp.log(df / max_exact) / math.log(REL_MAX_DIST / max_exact)
                         * (REL_BUCKETS - max_exact)).astype(jnp.int32)
    large = jnp.minimum(large, REL_BUCKETS - 1)
    return jnp.where(dist < max_exact, dist, large)


def to_heads(t, n_heads):
    b, s, _ = t.shape
    return t.reshape(b, s, n_heads, HEAD_DIM).transpose(0, 2, 1, 3)


def gmlp_branch(u, v, ln_g, ln_b, w_s, b_s):
    b, s, _ = u.shape
    nc = s // GMLP_CHUNK
    u = jax.nn.gelu(u)
    v = jax.nn.gelu(v).reshape(b, nc, GMLP_CHUNK, GMLP_GROUPS, HEAD_DIM)
    vf = v.astype(jnp.float32)
    mu = jnp.mean(vf, axis=-1, keepdims=True)
    var = jnp.mean(jnp.square(vf - mu), axis=-1, keepdims=True)
    vn = ((vf - mu) * lax.rsqrt(var + EPS) * ln_g.reshape(GMLP_GROUPS, HEAD_DIM)
          + ln_b.reshape(GMLP_GROUPS, HEAD_DIM)).astype(u.dtype)
    causal = jnp.tril(jnp.ones((GMLP_CHUNK, GMLP_CHUNK), dtype=w_s.dtype))
    w = w_s * causal[None]
    sv = jnp.einsum('gts,bcsgd->bctgd', w, vn) + b_s.T[None, None, :, :, None]
    return (u.reshape(b, nc, GMLP_CHUNK, GMLP_GROUPS, HEAD_DIM) * sv).reshape(b, s, D_GMLP)


def moba_branch(q, k, v, rel_bias):
    b, s, _ = q.shape
    h = MOBA_HEADS
    nb = -(-s // MOBA_BLOCK)
    pad = nb * MOBA_BLOCK - s
    k_sel = min(MOBA_TOPK, max(nb - 1, 1))
    qh = to_heads(q, h) * (HEAD_DIM ** -0.5)
    kh = jnp.pad(to_heads(k, h), ((0, 0), (0, 0), (0, pad), (0, 0)))
    vh = jnp.pad(to_heads(v, h), ((0, 0), (0, 0), (0, pad), (0, 0)))
    kb = kh.reshape(b, h, nb, MOBA_BLOCK, HEAD_DIM)
    vb = vh.reshape(b, h, nb, MOBA_BLOCK, HEAD_DIM)
    kmean = jnp.mean(kb.astype(jnp.float32), axis=3).astype(kb.dtype)
    n_qc = s // MOBA_Q_CHUNK
    qc = qh.reshape(b, h, n_qc, MOBA_Q_CHUNK, HEAD_DIM).transpose(2, 0, 1, 3, 4)
    table_t = rel_bias.T.astype(jnp.float32)
    bi = jnp.arange(b)[:, None, None, None]
    hi = jnp.arange(h)[None, :, None, None]
    blk_off = jnp.arange(MOBA_BLOCK)

    def step(args):
        c, qb = args
        q_pos = c * MOBA_Q_CHUNK + jnp.arange(MOBA_Q_CHUNK)
        blk = (c * MOBA_Q_CHUNK) // MOBA_BLOCK
        gs = jnp.einsum('bhqd,bhnd->bhqn', qb, kmean).astype(jnp.float32)
        gs = jnp.where(jnp.arange(nb) < blk, gs, NEG)
        _, sel = lax.top_k(gs, k_sel)
        valid = sel < blk
        kg = kb[bi, hi, sel]
        vg = vb[bi, hi, sel]
        l_past = jnp.einsum('bhqd,bhqnkd->bhqnk', qb, kg).astype(jnp.float32)
        past_pos = sel[..., None] * MOBA_BLOCK + blk_off
        past_bkt = t5_bucket(q_pos[None, None, :, None, None] - past_pos)
        l_past = l_past + table_t[hi[..., None], past_bkt]
        l_past = jnp.where(valid[..., None], l_past, NEG)
        k_own = lax.dynamic_index_in_dim(kb, blk, axis=2, keepdims=False)
        v_own = lax.dynamic_index_in_dim(vb, blk, axis=2, keepdims=False)
        own_pos = blk * MOBA_BLOCK + blk_off
        dist = q_pos[:, None] - own_pos[None, :]
        l_own = jnp.einsum('bhqd,bhkd->bhqk', qb, k_own).astype(jnp.float32)
        l_own = l_own + table_t[:, t5_bucket(dist)][None]
        l_own = jnp.where((dist >= 0)[None, None], l_own, NEG)
        logits = jnp.concatenate(
            [l_past.reshape(b, h, MOBA_Q_CHUNK, k_sel * MOBA_BLOCK), l_own], axis=-1)
        p = jax.nn.softmax(logits, axis=-1).astype(vb.dtype)
        p_past = p[..., :k_sel * MOBA_BLOCK].reshape(b, h, MOBA_Q_CHUNK, k_sel, MOBA_BLOCK)
        p_own = p[..., k_sel * MOBA_BLOCK:]
        return (jnp.einsum('bhqnk,bhqnkd->bhqd', p_past, vg)
                + jnp.einsum('bhqk,bhkd->bhqd', p_own, v_own))

    out = lax.map(step, (jnp.arange(n_qc), qc))
    return out.transpose(1, 0, 3, 2, 4).reshape(b, s, D_MOBA)


def memory_branch(q, mem_n, w_kv):
    b, s, _ = q.shape
    k, v = jnp.split(mem_n @ w_kv, 2, axis=-1)
    qh = to_heads(q, MEM_HEADS) * (HEAD_DIM ** -0.5)
    kh = to_heads(k, MEM_HEADS)
    vh = to_heads(v, MEM_HEADS)
    logits = jnp.einsum('bhqd,bhkd->bhqk', qh, kh).astype(jnp.float32)
    p = jax.nn.softmax(logits, axis=-1).astype(vh.dtype)
    o = jnp.einsum('bhqk,bhkd->bhqd', p, vh)
    return o.transpose(0, 2, 1, 3).reshape(b, s, D_MEM)


def setup_inputs(seed: int = 0) -> dict:
    key = jax.random.key(seed)
    ks = jax.random.split(key, 16)
    nrm = jax.random.normal
    f32 = jnp.float32
    return {
        "x": nrm(ks[0], (BATCH, SEQ, D_MODEL), f32),
        "mem": nrm(ks[1], (BATCH, MEM_LEN, D_MODEL), f32),
        "norm_g": 1.0 + 0.02 * nrm(ks[2], (DEPTH, D_MODEL), f32),
        "w_in": nrm(ks[3], (DEPTH, D_MODEL, D_IN), f32) * D_MODEL ** -0.5,
        "gmlp_ln_g": 1.0 + 0.02 * nrm(ks[4], (DEPTH, D_GMLP), f32),
        "gmlp_ln_b": 0.02 * nrm(ks[5], (DEPTH, D_GMLP), f32),
        "spatial_w": nrm(ks[6], (DEPTH, GMLP_GROUPS, GMLP_CHUNK, GMLP_CHUNK), f32) * GMLP_CHUNK ** -0.5,
        "spatial_b": 1.0 + 0.1 * nrm(ks[7], (DEPTH, GMLP_GROUPS, GMLP_CHUNK), f32),
        "rel_bias": 0.2 * nrm(ks[8], (REL_BUCKETS, MOBA_HEADS), f32),
        "mem_norm_g": 1.0 + 0.02 * nrm(ks[9], (DEPTH, D_MODEL), f32),
        "w_mem_kv": nrm(ks[10], (DEPTH, D_MODEL, 2 * D_MEM), f32) * D_MODEL ** -0.5,
        "branch_norm_g": 1.0 + 0.02 * nrm(ks[11], (DEPTH, D_MIX), f32),
        "w_out": nrm(ks[12], (DEPTH, D_MIX, D_MODEL), f32) * (0.5 * D_MIX ** -0.5),
        "final_norm_g": 1.0 + 0.02 * nrm(ks[13], (D_MODEL,), f32),
    }


def reference(x, mem, norm_g, w_in, gmlp_ln_g, gmlp_ln_b, spatial_w, spatial_b, rel_bias,
              mem_norm_g, w_mem_kv, branch_norm_g, w_out, final_norm_g):
    split_at = list(np.cumsum([D_GMLP, D_GMLP, D_MOBA, D_MOBA, D_MOBA, D_MEM]))
    for l in range(DEPTH):
        h = rmsnorm(x, norm_g[l])
        proj = h @ w_in[l]
        u_a, v_a, q_b, k_b, v_b, q_m, z = jnp.split(proj, split_at, axis=-1)
        y_a = gmlp_branch(u_a, v_a, gmlp_ln_g[l], gmlp_ln_b[l], spatial_w[l], spatial_b[l])
        y_b = moba_branch(q_b, k_b, v_b, rel_bias)
        y_m = memory_branch(q_m, rmsnorm(mem, mem_norm_g[l]), w_mem_kv[l])
        g = branch_norm_g[l]
        y = jnp.concatenate([
            rmsnorm(y_a, g[:D_GMLP]),
            rmsnorm(y_b, g[D_GMLP:D_GMLP + D_MOBA]),
            rmsnorm(y_m, g[D_GMLP + D_MOBA:]),
        ], axis=-1) * jax.nn.silu(z)
        x = x + y @ w_out[l]
    return rmsnorm(x, final_norm_g)
```

```python
import functools
import math

import numpy as np
import jax
import jax.numpy as jnp
from jax import lax
from jax.experimental import pallas as pl
from jax.experimental.pallas import tpu as pltpu

D_MODEL = 1024
HEAD_DIM = 64
GMLP_GROUPS = 6
MOBA_HEADS = 6
MEM_HEADS = 4
D_GMLP = GMLP_GROUPS * HEAD_DIM
D_MOBA = MOBA_HEADS * HEAD_DIM
D_MEM = MEM_HEADS * HEAD_DIM
D_MIX = D_GMLP + D_MOBA + D_MEM
D_IN = 2 * D_GMLP + 3 * D_MOBA + D_MEM + D_MIX
GMLP_CHUNK = 128
MOBA_BLOCK = 256
MOBA_TOPK = 3
REL_BUCKETS = 32
REL_MAX_DIST = 128
EPS = 1e-6
NEG = -1e30

LANES = 128
SEL_ROWS = 16
VMEM_LIMIT = 48 * 1024 * 1024

F32 = jnp.float32
BF16 = jnp.bfloat16

_OFF = np.cumsum([0, D_GMLP, D_GMLP, D_MOBA, D_MOBA, D_MOBA, D_MEM, D_MIX])


def _params(n_axes):
    return pltpu.CompilerParams(
        dimension_semantics=("arbitrary",) * n_axes, vmem_limit_bytes=VMEM_LIMIT)


def _rms(x, g):
    return x * lax.rsqrt(jnp.mean(x * x, axis=-1, keepdims=True) + EPS) * g


def _dot(a, b):
    return jnp.dot(a, b, preferred_element_type=F32)


def _dot_nt(a, b):
    return lax.dot_general(a, b, (((1,), (1,)), ((), ())), preferred_element_type=F32)


def _mem_kv_kernel(mem_ref, g_ref, w_ref, km_ref, vmt_ref):
    mn = _rms(mem_ref[0], g_ref[...]).astype(BF16)
    kv = _dot(mn, w_ref[...])
    km_ref[0] = kv[:, :D_MEM].astype(BF16)
    vmt_ref[0] = kv[:, D_MEM:].T.astype(BF16)


def _mem_kv(mem, g, w):
    b, m, d = mem.shape
    return pl.pallas_call(
        _mem_kv_kernel,
        grid=(b,),
        in_specs=[pl.BlockSpec((1, m, d), lambda i: (i, 0, 0)),
                  pl.BlockSpec((1, d), lambda i: (0, 0)),
                  pl.BlockSpec((d, 2 * D_MEM), lambda i: (0, 0))],
        out_specs=[pl.BlockSpec((1, m, D_MEM), lambda i: (i, 0, 0)),
                   pl.BlockSpec((1, D_MEM, m), lambda i: (i, 0, 0))],
        out_shape=[jax.ShapeDtypeStruct((b, m, D_MEM), BF16),
                   jax.ShapeDtypeStruct((b, D_MEM, m), BF16)],
        compiler_params=_params(1),
        name="mem_kv",
    )(mem, g, w)


def _in_proj_kernel(x_ref, g_ref, w_ref, u_ref, v_ref, q_ref, k_ref, vt_ref, qm_ref, z_ref, vtmp_ref):
    h = _rms(x_ref[0], g_ref[...]).astype(BF16)

    def seg(i):
        return _dot(h, w_ref[:, _OFF[i]:_OFF[i + 1]])

    scale = HEAD_DIM ** -0.5
    u_ref[0] = seg(0).astype(BF16)
    v_ref[0] = seg(1).astype(BF16)
    q_ref[0] = (seg(2) * scale).astype(BF16)
    k_ref[0] = seg(3).astype(BF16)
    vtmp_ref[...] = seg(4)
    vt_ref[0] = vtmp_ref[...].T.astype(BF16)
    qm_ref[0] = (seg(5) * scale).astype(BF16)
    z_ref[0] = seg(6).astype(BF16)


def _in_proj(x, g, w, tm):
    b, s, d = x.shape
    row = lambda n: pl.BlockSpec((1, tm, n), lambda i, j: (i, j, 0))
    sds = lambda *shape: jax.ShapeDtypeStruct(shape, BF16)
    return pl.pallas_call(
        _in_proj_kernel,
        grid=(b, s // tm),
        in_specs=[row(d),
                  pl.BlockSpec((1, d), lambda i, j: (0, 0)),
                  pl.BlockSpec((d, D_IN), lambda i, j: (0, 0))],
        out_specs=[row(D_GMLP), row(D_GMLP), row(D_MOBA), row(D_MOBA),
                   pl.BlockSpec((1, D_MOBA, tm), lambda i, j: (i, 0, j)),
                   row(D_MEM), row(D_MIX)],
        out_shape=[sds(b, s, D_GMLP), sds(b, s, D_GMLP), sds(b, s, D_MOBA), sds(b, s, D_MOBA),
                   sds(b, D_MOBA, s), sds(b, s, D_MEM), sds(b, s, D_MIX)],
        scratch_shapes=[pltpu.VMEM((tm, D_MOBA), F32)],
        compiler_params=_params(2),
        name="in_proj",
    )(x, g, w)


def _split_dot(t, a):
    hi = t.astype(BF16)
    lo = (t - hi.astype(F32)).astype(BF16)
    return _dot(hi, a) + _dot(lo, a)


def _gmlp_kernel(u_ref, v_ref, lng_ref, lnb_ref, w_ref, sb_ref, avg_ref, bg_ref, o_ref, *, n_chunks):
    lane = lax.broadcasted_iota(jnp.int32, (GMLP_CHUNK, LANES), 1)
    first_head = lane < HEAD_DIM
    avg = avg_ref[...]
    for c in range(n_chunks):
        rows = slice(c * GMLP_CHUNK, (c + 1) * GMLP_CHUNK)
        uc = jax.nn.gelu(u_ref[rows, :].astype(F32))
        vc = jax.nn.gelu(v_ref[rows, :].astype(F32))
        mu = _split_dot(vc, avg)
        dv = vc - mu
        var = _split_dot(dv * dv, avg)
        vn = (dv * lax.rsqrt(var + EPS) * lng_ref[...] + lnb_ref[...]).astype(BF16)
        parts = []
        for p in range(GMLP_GROUPS // 2):
            vp = vn[:, p * LANES:(p + 1) * LANES]
            parts.append(jnp.where(first_head, _dot(w_ref[2 * p], vp), _dot(w_ref[2 * p + 1], vp)))
        sv = jnp.concatenate(parts, axis=-1) + sb_ref[...]
        o_ref[rows, :] = _rms(uc * sv, bg_ref[...]).astype(BF16)


def _gmlp(u, v, ln_g, ln_b, w_causal, sb_exp, avg, bg, tc):
    t, d = u.shape
    row = pl.BlockSpec((tc, d), lambda i: (i, 0))
    full = lambda a: pl.BlockSpec(a.shape, lambda i: (0,) * a.ndim)
    return pl.pallas_call(
        functools.partial(_gmlp_kernel, n_chunks=tc // GMLP_CHUNK),
        grid=(t // tc,),
        in_specs=[row, row, full(ln_g), full(ln_b), full(w_causal), full(sb_exp), full(avg), full(bg)],
        out_specs=row,
        out_shape=jax.ShapeDtypeStruct((t, d), BF16),
        compiler_params=_params(1),
        name="gmlp",
    )(u, v, ln_g, ln_b, w_causal, sb_exp, avg, bg)


def _head_lane_mask(parity):
    lane = lax.broadcasted_iota(jnp.int32, (MOBA_BLOCK, LANES), 1)
    return (lane < HEAD_DIM) if parity == 0 else (lane >= HEAD_DIM)


def _attn_kernel(q_ref, k_ref, vt_ref, qm_ref, km_ref, vmt_ref, bias_ref, bgb_ref, bgm_ref,
                 yb_ref, ym_ref, kmean_hi, kmean_lo, sel_ref, m_ref, l_ref, acc_ref, ot_ref, omt_ref,
                 *, n_blocks):
    t = pl.program_id(1)
    blk = MOBA_BLOCK

    @pl.when(t == 0)
    def _():
        kmean_hi[...] = jnp.zeros_like(kmean_hi)
        kmean_lo[...] = jnp.zeros_like(kmean_lo)
        for j in range(n_blocks):
            km = jnp.mean(k_ref[0, j * blk:(j + 1) * blk, :].astype(F32), axis=0, keepdims=True)
            hi = km.astype(BF16)
            kmean_hi[j:j + 1, :] = hi
            kmean_lo[j:j + 1, :] = (km - hi.astype(F32)).astype(BF16)

    row_id = lax.broadcasted_iota(jnp.int32, (SEL_ROWS, blk), 0)
    kpos = lax.broadcasted_iota(jnp.int32, (blk, blk), 0)
    qpos = lax.broadcasted_iota(jnp.int32, (blk, blk), 1)
    causal_t = kpos <= qpos
    t0 = pl.multiple_of(t * blk, blk)

    for h in range(MOBA_HEADS):
        lanes = slice((h // 2) * LANES, (h // 2 + 1) * LANES)
        q_h = jnp.where(_head_lane_mask(h % 2), q_ref[0, :, lanes], jnp.zeros((), BF16))

        gs = _dot_nt(kmean_hi[:, lanes], q_h) + _dot_nt(kmean_lo[:, lanes], q_h)
        gs = jnp.where(row_id < t, gs, NEG)
        rank = jnp.zeros((SEL_ROWS, blk), jnp.int32)
        for i in range(n_blocks):
            gi = gs[i:i + 1, :]
            ahead = (gi > gs) | ((gi == gs) & (row_id > i))
            rank = rank + ahead.astype(jnp.int32)
        sel_ref[...] = ((rank < MOBA_TOPK) & (row_id < t)).astype(F32)

        s = _dot_nt(k_ref[0, pl.ds(t0, blk), lanes], q_h)
        s = jnp.where(causal_t, s + bias_ref[h, 0], NEG)
        m0 = jnp.max(s, axis=0, keepdims=True)
        p = jnp.exp(s - m0)
        m_ref[...] = m0
        l_ref[...] = jnp.sum(p, axis=0, keepdims=True)
        acc_ref[...] = _dot(vt_ref[0, h * HEAD_DIM:(h + 1) * HEAD_DIM, pl.ds(t0, blk)], p.astype(BF16))

        def past(j, carry):
            j0 = pl.multiple_of(j * blk, blk)
            kind = jnp.where(j == t - 1, 1, 2)
            s = _dot_nt(k_ref[0, pl.ds(j0, blk), lanes], q_h)
            s = jnp.where(sel_ref[pl.ds(j, 1), :] > 0.5, s + bias_ref[h, kind], NEG)
            m_old = m_ref[...]
            m_new = jnp.maximum(m_old, jnp.max(s, axis=0, keepdims=True))
            alpha = jnp.exp(m_old - m_new)
            p = jnp.exp(s - m_new)
            m_ref[...] = m_new
            l_ref[...] = alpha * l_ref[...] + jnp.sum(p, axis=0, keepdims=True)
            acc_ref[...] = alpha * acc_ref[...] + _dot(
                vt_ref[0, h * HEAD_DIM:(h + 1) * HEAD_DIM, pl.ds(j0, blk)], p.astype(BF16))
            return carry

        lax.fori_loop(0, t, past, 0)
        ot_ref[h * HEAD_DIM:(h + 1) * HEAD_DIM, :] = acc_ref[...] / l_ref[...]

    yb_ref[0] = _rms(ot_ref[...].T, bgb_ref[...]).astype(BF16)

    for h in range(MEM_HEADS):
        lanes = slice((h // 2) * LANES, (h // 2 + 1) * LANES)
        q_h = jnp.where(_head_lane_mask(h % 2), qm_ref[0, :, lanes], jnp.zeros((), BF16))
        s = _dot_nt(km_ref[0, :, lanes], q_h)
        p = jnp.exp(s - jnp.max(s, axis=0, keepdims=True))
        o = _dot(vmt_ref[0, h * HEAD_DIM:(h + 1) * HEAD_DIM, :], p.astype(BF16))
        omt_ref[h * HEAD_DIM:(h + 1) * HEAD_DIM, :] = o / jnp.sum(p, axis=0, keepdims=True)

    ym_ref[0] = _rms(omt_ref[...].T, bgm_ref[...]).astype(BF16)


def _attention(q, k, vt, qm, km, vmt, bias, bgb, bgm):
    b, s, _ = q.shape
    m = km.shape[1]
    nb = s // MOBA_BLOCK
    blk = MOBA_BLOCK
    full = lambda a: pl.BlockSpec(a.shape, lambda i, j: (0,) * a.ndim)
    return pl.pallas_call(
        functools.partial(_attn_kernel, n_blocks=nb),
        grid=(b, nb),
        in_specs=[pl.BlockSpec((1, blk, D_MOBA), lambda i, j: (i, j, 0)),
                  pl.BlockSpec((1, s, D_MOBA), lambda i, j: (i, 0, 0)),
                  pl.BlockSpec((1, D_MOBA, s), lambda i, j: (i, 0, 0)),
                  pl.BlockSpec((1, blk, D_MEM), lambda i, j: (i, j, 0)),
                  pl.BlockSpec((1, m, D_MEM), lambda i, j: (i, 0, 0)),
                  pl.BlockSpec((1, D_MEM, m), lambda i, j: (i, 0, 0)),
                  full(bias), full(bgb), full(bgm)],
        out_specs=[pl.BlockSpec((1, blk, D_MOBA), lambda i, j: (i, j, 0)),
                   pl.BlockSpec((1, blk, D_MEM), lambda i, j: (i, j, 0))],
        out_shape=[jax.ShapeDtypeStruct((b, s, D_MOBA), BF16),
                   jax.ShapeDtypeStruct((b, s, D_MEM), BF16)],
        scratch_shapes=[pltpu.VMEM((SEL_ROWS, D_MOBA), BF16),
                        pltpu.VMEM((SEL_ROWS, D_MOBA), BF16),
                        pltpu.VMEM((SEL_ROWS, blk), F32),
                        pltpu.VMEM((1, blk), F32),
                        pltpu.VMEM((1, blk), F32),
                        pltpu.VMEM((HEAD_DIM, blk), F32),
                        pltpu.VMEM((D_MOBA, blk), F32),
                        pltpu.VMEM((D_MEM, blk), F32)],
        compiler_params=_params(2),
        name="attention",
    )(q, k, vt, qm, km, vmt, bias, bgb, bgm)


def _out_proj_kernel(ya_ref, yb_ref, ym_ref, z_ref, x_ref, w_ref, fg_ref, o_ref, *, final):
    y = jnp.concatenate([ya_ref[...], yb_ref[...], ym_ref[...]], axis=-1).astype(F32)
    y = (y * jax.nn.silu(z_ref[...].astype(F32))).astype(BF16)
    out = x_ref[...] + _dot(y, w_ref[...])
    if final:
        out = _rms(out, fg_ref[...])
    o_ref[...] = out


def _out_proj(ya, yb, ym, z, x, w, fg, tm, final):
    t, d = x.shape
    row = lambda n: pl.BlockSpec((tm, n), lambda i: (i, 0))
    return pl.pallas_call(
        functools.partial(_out_proj_kernel, final=final),
        grid=(t // tm,),
        in_specs=[row(D_GMLP), row(D_MOBA), row(D_MEM), row(D_MIX), row(d),
                  pl.BlockSpec(w.shape, lambda i: (0, 0)),
                  pl.BlockSpec(fg.shape, lambda i: (0, 0))],
        out_specs=row(d),
        out_shape=jax.ShapeDtypeStruct((t, d), F32),
        compiler_params=_params(1),
        name="out_proj",
    )(ya, yb, ym, z, x, w, fg)


def _t5_bucket(dist):
    dist = jnp.maximum(dist, 0)
    max_exact = REL_BUCKETS // 2
    df = jnp.maximum(dist, 1).astype(F32)
    large = max_exact + (jnp.log(df / max_exact) / math.log(REL_MAX_DIST / max_exact)
                         * (REL_BUCKETS - max_exact)).astype(jnp.int32)
    large = jnp.minimum(large, REL_BUCKETS - 1)
    return jnp.where(dist < max_exact, dist, large)


def _bias_tables(rel_bias, seq):
    kq = jnp.arange(MOBA_BLOCK)
    d_own = kq[None, :] - kq[:, None]
    far_lo, far_hi = MOBA_BLOCK + 1, seq
    assert REL_MAX_DIST <= far_lo
    buckets = jnp.stack([_t5_bucket(d_own), _t5_bucket(d_own + MOBA_BLOCK),
                         jnp.full_like(d_own, REL_BUCKETS - 1)])
    table_t = rel_bias.T.astype(F32)
    return table_t[:, buckets]


def kernel(x, mem, norm_g, w_in, gmlp_ln_g, gmlp_ln_b, spatial_w, spatial_b, rel_bias,
           mem_norm_g, w_mem_kv, branch_norm_g, w_out, final_norm_g):
    b, s, d = x.shape
    depth = w_in.shape[0]
    t = b * s
    tm = 512

    bias = _bias_tables(rel_bias, s)
    causal = jnp.tril(jnp.ones((GMLP_CHUNK, GMLP_CHUNK), F32))
    grp = np.arange(D_GMLP) // HEAD_DIM
    avg = jnp.asarray((grp[:, None] == grp[None, :]).astype(np.float32) / HEAD_DIM, BF16)
    fg = final_norm_g.reshape(1, d)

    for l in range(depth):
        km, vmt = _mem_kv(mem, mem_norm_g[l].reshape(1, d), w_mem_kv[l].astype(BF16))
        u, v, q, k, vt, qm, z = _in_proj(x, norm_g[l].reshape(1, d), w_in[l].astype(BF16), tm)
        bg = branch_norm_g[l]
        sb_exp = jnp.repeat(spatial_b[l].T, HEAD_DIM, axis=1)
        ya = _gmlp(u.reshape(t, D_GMLP), v.reshape(t, D_GMLP),
                   gmlp_ln_g[l].reshape(1, D_GMLP), gmlp_ln_b[l].reshape(1, D_GMLP),
                   (spatial_w[l] * causal[None]).astype(BF16), sb_exp, avg,
                   bg[:D_GMLP].reshape(1, D_GMLP), tm)
        yb, ym = _attention(q, k, vt, qm, km, vmt, bias,
                            bg[D_GMLP:D_GMLP + D_MOBA].reshape(1, D_MOBA),
                            bg[D_GMLP + D_MOBA:].reshape(1, D_MEM))
        x = _out_proj(ya, yb.reshape(t, D_MOBA), ym.reshape(t, D_MEM), z.reshape(t, D_MIX),
                      x.reshape(t, d), w_out[l].astype(BF16), fg, tm,
                      final=(l == depth - 1)).reshape(b, s, d)
    return x
```

```python
import functools
import math

import numpy as np
import jax
import jax.numpy as jnp
from jax import lax
from jax.experimental import pallas as pl
from jax.experimental.pallas import tpu as pltpu

D_MODEL = 1024
HEAD_DIM = 64
GMLP_GROUPS = 6
MOBA_HEADS = 6
MEM_HEADS = 4
D_GMLP = GMLP_GROUPS * HEAD_DIM
D_MOBA = MOBA_HEADS * HEAD_DIM
D_MEM = MEM_HEADS * HEAD_DIM
D_MIX = D_GMLP + D_MOBA + D_MEM
D_IN = 2 * D_GMLP + 3 * D_MOBA + D_MEM + D_MIX
GMLP_CHUNK = 128
MOBA_BLOCK = 256
MOBA_TOPK = 3
REL_BUCKETS = 32
REL_MAX_DIST = 128
EPS = 1e-6
NEG = -1e30
LOG2E = math.log2(math.e)

LANES = 128
SEL_ROWS = 16
VMEM_LIMIT = 48 * 1024 * 1024

F32 = jnp.float32
BF16 = jnp.bfloat16

_OFF = np.cumsum([0, D_GMLP, D_GMLP, D_MOBA, D_MOBA, D_MOBA, D_MEM, D_MIX])


def _params(n_axes):
    return pltpu.CompilerParams(
        dimension_semantics=("arbitrary",) * n_axes, vmem_limit_bytes=VMEM_LIMIT)


def _rms(x, g):
    return x * lax.rsqrt(jnp.mean(x * x, axis=-1, keepdims=True) + EPS) * g


def _dot(a, b):
    return jnp.dot(a, b, preferred_element_type=F32)


def _dot_nt(a, b):
    return lax.dot_general(a, b, (((1,), (1,)), ((), ())), preferred_element_type=F32)


def _mem_kv_kernel(mem_ref, g_ref, w_ref, km_ref, vmt_ref):
    mn = _rms(mem_ref[0], g_ref[...]).astype(BF16)
    kv = _dot(mn, w_ref[...])
    km_ref[0] = kv[:, :D_MEM].astype(BF16)
    vmt_ref[0] = kv[:, D_MEM:].T.astype(BF16)


def _mem_kv(mem, g, w):
    b, m, d = mem.shape
    return pl.pallas_call(
        _mem_kv_kernel,
        grid=(b,),
        in_specs=[pl.BlockSpec((1, m, d), lambda i: (i, 0, 0)),
                  pl.BlockSpec((1, d), lambda i: (0, 0)),
                  pl.BlockSpec((d, 2 * D_MEM), lambda i: (0, 0))],
        out_specs=[pl.BlockSpec((1, m, D_MEM), lambda i: (i, 0, 0)),
                   pl.BlockSpec((1, D_MEM, m), lambda i: (i, 0, 0))],
        out_shape=[jax.ShapeDtypeStruct((b, m, D_MEM), BF16),
                   jax.ShapeDtypeStruct((b, D_MEM, m), BF16)],
        compiler_params=_params(1),
        name="mem_kv",
    )(mem, g, w)


def _in_proj_kernel(x_ref, g_ref, w_ref, u_ref, v_ref, q_ref, k_ref, vt_ref, qm_ref, z_ref, vtmp_ref):
    h = _rms(x_ref[0], g_ref[...]).astype(BF16)

    def seg(i):
        return _dot(h, w_ref[:, _OFF[i]:_OFF[i + 1]])

    scale = HEAD_DIM ** -0.5 * LOG2E
    u_ref[0] = seg(0).astype(BF16)
    v_ref[0] = seg(1).astype(BF16)
    q_ref[0] = (seg(2) * scale).astype(BF16)
    k_ref[0] = seg(3).astype(BF16)
    vtmp_ref[...] = seg(4)
    vt_ref[0] = vtmp_ref[...].T.astype(BF16)
    qm_ref[0] = (seg(5) * scale).astype(BF16)
    z_ref[0] = seg(6).astype(BF16)


def _in_proj(x, g, w, tm):
    b, s, d = x.shape
    row = lambda n: pl.BlockSpec((1, tm, n), lambda i, j: (i, j, 0))
    sds = lambda *shape: jax.ShapeDtypeStruct(shape, BF16)
    return pl.pallas_call(
        _in_proj_kernel,
        grid=(b, s // tm),
        in_specs=[row(d),
                  pl.BlockSpec((1, d), lambda i, j: (0, 0)),
                  pl.BlockSpec((d, D_IN), lambda i, j: (0, 0))],
        out_specs=[row(D_GMLP), row(D_GMLP), row(D_MOBA), row(D_MOBA),
                   pl.BlockSpec((1, D_MOBA, tm), lambda i, j: (i, 0, j)),
                   row(D_MEM), row(D_MIX)],
        out_shape=[sds(b, s, D_GMLP), sds(b, s, D_GMLP), sds(b, s, D_MOBA), sds(b, s, D_MOBA),
                   sds(b, D_MOBA, s), sds(b, s, D_MEM), sds(b, s, D_MIX)],
        scratch_shapes=[pltpu.VMEM((tm, D_MOBA), F32)],
        compiler_params=_params(2),
        name="in_proj",
    )(x, g, w)


def _gmlp_kernel(u_ref, v_ref, lng_ref, lnb_ref, w_ref, sb_ref, avg_ref, bg_ref, o_ref, *, n_chunks):
    lane = lax.broadcasted_iota(jnp.int32, (GMLP_CHUNK, LANES), 1)
    first_head = lane < HEAD_DIM
    avg = avg_ref[...]
    for c in range(n_chunks):
        rows = slice(c * GMLP_CHUNK, (c + 1) * GMLP_CHUNK)
        uc = jax.nn.gelu(u_ref[rows, :].astype(F32))
        vc = jax.nn.gelu(v_ref[rows, :].astype(F32))
        mu = _dot(vc.astype(BF16), avg)
        dv = vc - mu
        var = _dot((dv * dv).astype(BF16), avg)
        vn = (dv * lax.rsqrt(var + EPS) * lng_ref[...] + lnb_ref[...]).astype(BF16)
        parts = []
        for p in range(GMLP_GROUPS // 2):
            vp = vn[:, p * LANES:(p + 1) * LANES]
            parts.append(jnp.where(first_head, _dot(w_ref[2 * p], vp), _dot(w_ref[2 * p + 1], vp)))
        sv = jnp.concatenate(parts, axis=-1) + sb_ref[...]
        o_ref[rows, :] = _rms(uc * sv, bg_ref[...]).astype(BF16)


def _gmlp(u, v, ln_g, ln_b, w_causal, sb_exp, avg, bg, tc):
    t, d = u.shape
    row = pl.BlockSpec((tc, d), lambda i: (i, 0))
    full = lambda a: pl.BlockSpec(a.shape, lambda i: (0,) * a.ndim)
    return pl.pallas_call(
        functools.partial(_gmlp_kernel, n_chunks=tc // GMLP_CHUNK),
        grid=(t // tc,),
        in_specs=[row, row, full(ln_g), full(ln_b), full(w_causal), full(sb_exp), full(avg), full(bg)],
        out_specs=row,
        out_shape=jax.ShapeDtypeStruct((t, d), BF16),
        compiler_params=_params(1),
        name="gmlp",
    )(u, v, ln_g, ln_b, w_causal, sb_exp, avg, bg)


def _head_lanes(h):
    return slice((h // 2) * LANES, (h // 2 + 1) * LANES)


def _head_rows(h):
    return slice(h * HEAD_DIM, (h + 1) * HEAD_DIM)


def _isolate_head(pair, h):
    lane = lax.broadcasted_iota(jnp.int32, pair.shape, 1)
    keep = (lane < HEAD_DIM) if h % 2 == 0 else (lane >= HEAD_DIM)
    return jnp.where(keep, pair, jnp.zeros((), pair.dtype))


def _attn_kernel(q_ref, k_ref, vt_ref, qm_ref, km_ref, vmt_ref, bias_ref, cfar_ref, bgb_ref, bgm_ref,
                 yb_ref, ym_ref, kmean_hi, kmean_lo, qh_ref, sel_ref, m_ref, l_ref, ot_ref, omt_ref,
                 sa_ref, sb_ref, *, n_blocks):
    t = pl.program_id(1)
    blk = MOBA_BLOCK

    @pl.when(t == 0)
    def _():
        kmean_hi[...] = jnp.zeros_like(kmean_hi)
        kmean_lo[...] = jnp.zeros_like(kmean_lo)
        for j in range(n_blocks):
            km = jnp.mean(k_ref[0, j * blk:(j + 1) * blk, :].astype(F32), axis=0, keepdims=True)
            hi = km.astype(BF16)
            kmean_hi[j:j + 1, :] = hi
            kmean_lo[j:j + 1, :] = (km - hi.astype(F32)).astype(BF16)

    row_id = lax.broadcasted_iota(jnp.int32, (SEL_ROWS, blk), 0)
    kpos = lax.broadcasted_iota(jnp.int32, (blk, blk), 0)
    qpos = lax.broadcasted_iota(jnp.int32, (blk, blk), 1)
    causal_t = kpos <= qpos

    def qk(j, h):
        j0 = pl.multiple_of(j * blk, blk)
        return _dot_nt(k_ref[0, pl.ds(j0, blk), _head_lanes(h)], qh_ref[h])

    def update(h, s, shift, j):
        j0 = pl.multiple_of(j * blk, blk)
        m_old = m_ref[h]
        m_new = jnp.maximum(m_old, jnp.max(s, axis=0, keepdims=True) + shift)
        alpha = jnp.exp2(m_old - m_new)
        p = jnp.exp2(s - (m_new - shift))
        m_ref[h] = m_new
        l_ref[h] = alpha * l_ref[h] + jnp.sum(p, axis=0, keepdims=True)
        ot_ref[_head_rows(h), :] = alpha * ot_ref[_head_rows(h), :] + _dot(
            vt_ref[0, _head_rows(h), pl.ds(j0, blk)], p.astype(BF16))

    for h in range(MOBA_HEADS):
        q_h = _isolate_head(q_ref[0, :, _head_lanes(h)], h)
        qh_ref[h] = q_h
        gs = _dot_nt(kmean_hi[:, _head_lanes(h)], q_h) + _dot_nt(kmean_lo[:, _head_lanes(h)], q_h)
        gs = jnp.where(row_id < t, gs, NEG)
        rank = jnp.zeros((SEL_ROWS, blk), jnp.int32)
        for i in range(n_blocks):
            gi = gs[i:i + 1, :]
            ahead = (gi > gs) | ((gi == gs) & (row_id > i))
            rank = rank + ahead.astype(jnp.int32)
        sel_ref[h] = ((rank < MOBA_TOPK) & (row_id < t)).astype(F32)

    s_own = [qk(t, h) for h in range(MOBA_HEADS)]
    for h in range(MOBA_HEADS):
        sa_ref[h] = qk(0, h)
        s = jnp.where(causal_t, s_own[h] + bias_ref[h, 0], NEG)
        m0 = jnp.max(s, axis=0, keepdims=True)
        p = jnp.exp2(s - m0)
        m_ref[h] = m0
        l_ref[h] = jnp.sum(p, axis=0, keepdims=True)
        t0 = pl.multiple_of(t * blk, blk)
        ot_ref[_head_rows(h), :] = _dot(vt_ref[0, _head_rows(h), pl.ds(t0, blk)], p.astype(BF16))

    def far_step(j, cur_ref, nxt_ref):
        for h in range(MOBA_HEADS):
            nxt_ref[h] = qk(j + 1, h)
            s = jnp.where(sel_ref[h, pl.ds(j, 1), :] > 0.5, cur_ref[h], NEG)
            update(h, s, cfar_ref[h], j)

    def far(j, carry):
        @pl.when(j % 2 == 0)
        def _():
            far_step(j, sa_ref, sb_ref)

        @pl.when(j % 2 == 1)
        def _():
            far_step(j, sb_ref, sa_ref)

        return carry

    lax.fori_loop(0, t - 1, far, 0)

    def prev_step(cur_ref):
        j = t - 1
        for h in range(MOBA_HEADS):
            s = jnp.where(sel_ref[h, pl.ds(j, 1), :] > 0.5, cur_ref[h] + bias_ref[h, 1], NEG)
            update(h, s, 0.0, j)

    @pl.when((t >= 1) & (t % 2 == 1))
    def _():
        prev_step(sa_ref)

    @pl.when((t >= 1) & (t % 2 == 0))
    def _():
        prev_step(sb_ref)

    for h in range(MOBA_HEADS):
        ot_ref[_head_rows(h), :] = ot_ref[_head_rows(h), :] * (1.0 / l_ref[h])
    yb_ref[0] = _rms(ot_ref[...].T, bgb_ref[...]).astype(BF16)

    s_mem = []
    for h in range(MEM_HEADS):
        q_h = _isolate_head(qm_ref[0, :, _head_lanes(h)], h)
        s_mem.append(_dot_nt(km_ref[0, :, _head_lanes(h)], q_h))
    for h in range(MEM_HEADS):
        s = s_mem[h]
        p = jnp.exp2(s - jnp.max(s, axis=0, keepdims=True))
        o = _dot(vmt_ref[0, _head_rows(h), :], p.astype(BF16))
        omt_ref[_head_rows(h), :] = o * (1.0 / jnp.sum(p, axis=0, keepdims=True))

    ym_ref[0] = _rms(omt_ref[...].T, bgm_ref[...]).astype(BF16)


def _attention(q, k, vt, qm, km, vmt, bias, cfar, bgb, bgm):
    b, s, _ = q.shape
    m = km.shape[1]
    nb = s // MOBA_BLOCK
    blk = MOBA_BLOCK
    full = lambda a: pl.BlockSpec(a.shape, lambda i, j: (0,) * a.ndim)
    return pl.pallas_call(
        functools.partial(_attn_kernel, n_blocks=nb),
        grid=(b, nb),
        in_specs=[pl.BlockSpec((1, blk, D_MOBA), lambda i, j: (i, j, 0)),
                  pl.BlockSpec((1, s, D_MOBA), lambda i, j: (i, 0, 0)),
                  pl.BlockSpec((1, D_MOBA, s), lambda i, j: (i, 0, 0)),
                  pl.BlockSpec((1, blk, D_MEM), lambda i, j: (i, j, 0)),
                  pl.BlockSpec((1, m, D_MEM), lambda i, j: (i, 0, 0)),
                  pl.BlockSpec((1, D_MEM, m), lambda i, j: (i, 0, 0)),
                  full(bias), full(cfar), full(bgb), full(bgm)],
        out_specs=[pl.BlockSpec((1, blk, D_MOBA), lambda i, j: (i, j, 0)),
                   pl.BlockSpec((1, blk, D_MEM), lambda i, j: (i, j, 0))],
        out_shape=[jax.ShapeDtypeStruct((b, s, D_MOBA), BF16),
                   jax.ShapeDtypeStruct((b, s, D_MEM), BF16)],
        scratch_shapes=[pltpu.VMEM((SEL_ROWS, D_MOBA), BF16),
                        pltpu.VMEM((SEL_ROWS, D_MOBA), BF16),
                        pltpu.VMEM((MOBA_HEADS, blk, LANES), BF16),
                        pltpu.VMEM((MOBA_HEADS, SEL_ROWS, blk), F32),
                        pltpu.VMEM((MOBA_HEADS, 1, blk), F32),
                        pltpu.VMEM((MOBA_HEADS, 1, blk), F32),
                        pltpu.VMEM((D_MOBA, blk), F32),
                        pltpu.VMEM((D_MEM, blk), F32),
                        pltpu.VMEM((MOBA_HEADS, blk, blk), F32),
                        pltpu.VMEM((MOBA_HEADS, blk, blk), F32)],
        compiler_params=_params(2),
        name="attention",
    )(q, k, vt, qm, km, vmt, bias, cfar, bgb, bgm)


def _out_proj_kernel(ya_ref, yb_ref, ym_ref, z_ref, x_ref, w_ref, fg_ref, o_ref, *, final):
    y = jnp.concatenate([ya_ref[...], yb_ref[...], ym_ref[...]], axis=-1).astype(F32)
    y = (y * jax.nn.silu(z_ref[...].astype(F32))).astype(BF16)
    out = x_ref[...] + _dot(y, w_ref[...])
    if final:
        out = _rms(out, fg_ref[...])
    o_ref[...] = out


def _out_proj(ya, yb, ym, z, x, w, fg, tm, final):
    t, d = x.shape
    row = lambda n: pl.BlockSpec((tm, n), lambda i: (i, 0))
    return pl.pallas_call(
        functools.partial(_out_proj_kernel, final=final),
        grid=(t // tm,),
        in_specs=[row(D_GMLP), row(D_MOBA), row(D_MEM), row(D_MIX), row(d),
                  pl.BlockSpec(w.shape, lambda i: (0, 0)),
                  pl.BlockSpec(fg.shape, lambda i: (0, 0))],
        out_specs=row(d),
        out_shape=jax.ShapeDtypeStruct((t, d), F32),
        compiler_params=_params(1),
        name="out_proj",
    )(ya, yb, ym, z, x, w, fg)


def _t5_bucket(dist):
    dist = jnp.maximum(dist, 0)
    max_exact = REL_BUCKETS // 2
    df = jnp.maximum(dist, 1).astype(F32)
    large = max_exact + (jnp.log(df / max_exact) / math.log(REL_MAX_DIST / max_exact)
                         * (REL_BUCKETS - max_exact)).astype(jnp.int32)
    large = jnp.minimum(large, REL_BUCKETS - 1)
    return jnp.where(dist < max_exact, dist, large)


def _bias_tables(rel_bias):
    assert MOBA_BLOCK + 1 >= REL_MAX_DIST
    kq = jnp.arange(MOBA_BLOCK)
    d_own = kq[None, :] - kq[:, None]
    buckets = jnp.stack([_t5_bucket(d_own), _t5_bucket(d_own + MOBA_BLOCK)])
    onehot = (buckets[None] == jnp.arange(REL_BUCKETS)[:, None, None, None]).astype(F32)
    table = rel_bias.astype(F32) * LOG2E
    tiles = jnp.einsum('nh,nxkq->hxkq', table, onehot, precision=lax.Precision.HIGHEST)
    cfar = jnp.broadcast_to(table[REL_BUCKETS - 1][:, None, None], (MOBA_HEADS, 1, MOBA_BLOCK))
    return tiles, cfar


def kernel(x, mem, norm_g, w_in, gmlp_ln_g, gmlp_ln_b, spatial_w, spatial_b, rel_bias,
           mem_norm_g, w_mem_kv, branch_norm_g, w_out, final_norm_g):
    b, s, d = x.shape
    depth = w_in.shape[0]
    t = b * s
    tm = 512

    bias, cfar = _bias_tables(rel_bias)
    causal = jnp.tril(jnp.ones((GMLP_CHUNK, GMLP_CHUNK), F32))
    grp = np.arange(D_GMLP) // HEAD_DIM
    avg = jnp.asarray((grp[:, None] == grp[None, :]).astype(np.float32) / HEAD_DIM, BF16)
    fg = final_norm_g.reshape(1, d)

    for l in range(depth):
        km, vmt = _mem_kv(mem, mem_norm_g[l].reshape(1, d), w_mem_kv[l].astype(BF16))
        u, v, q, k, vt, qm, z = _in_proj(x, norm_g[l].reshape(1, d), w_in[l].astype(BF16), tm)
        bg = branch_norm_g[l]
        sb_exp = jnp.repeat(spatial_b[l].T, HEAD_DIM, axis=1)
        ya = _gmlp(u.reshape(t, D_GMLP), v.reshape(t, D_GMLP),
                   gmlp_ln_g[l].reshape(1, D_GMLP), gmlp_ln_b[l].reshape(1, D_GMLP),
                   (spatial_w[l] * causal[None]).astype(BF16), sb_exp, avg,
                   bg[:D_GMLP].reshape(1, D_GMLP), tm)
        yb, ym = _attention(q, k, vt, qm, km, vmt, bias, cfar,
                            bg[D_GMLP:D_GMLP + D_MOBA].reshape(1, D_MOBA),
                            bg[D_GMLP + D_MOBA:].reshape(1, D_MEM))
        x = _out_proj(ya, yb.reshape(t, D_MOBA), ym.reshape(t, D_MEM), z.reshape(t, D_MIX),
                      x.reshape(t, d), w_out[l].astype(BF16), fg, tm,
                      final=(l == depth - 1)).reshape(b, s, d)
    return x
```

```python
import functools
import math

import numpy as np
import jax
import jax.numpy as jnp
from jax import lax
from jax.experimental import pallas as pl
from jax.experimental.pallas import tpu as pltpu

D_MODEL = 1024
HEAD_DIM = 64
GMLP_GROUPS = 6
MOBA_HEADS = 6
MEM_HEADS = 4
D_GMLP = GMLP_GROUPS * HEAD_DIM
D_MOBA = MOBA_HEADS * HEAD_DIM
D_MEM = MEM_HEADS * HEAD_DIM
D_MIX = D_GMLP + D_MOBA + D_MEM
D_IN = 2 * D_GMLP + 3 * D_MOBA + D_MEM + D_MIX
GMLP_CHUNK = 128
MOBA_BLOCK = 256
MOBA_TOPK = 3
REL_BUCKETS = 32
REL_MAX_DIST = 128
EPS = 1e-6
NEG = -1e30
LOG2E = math.log2(math.e)

LANES = 128
SEL_ROWS = 16
VMEM_LIMIT = 48 * 1024 * 1024

F32 = jnp.float32
BF16 = jnp.bfloat16

_OFF = np.cumsum([0, D_GMLP, D_GMLP, D_MOBA, D_MOBA, D_MOBA, D_MEM, D_MIX])


def _params(n_axes):
    return pltpu.CompilerParams(
        dimension_semantics=("arbitrary",) * n_axes, vmem_limit_bytes=VMEM_LIMIT)


def _rms(x, g):
    return x * lax.rsqrt(jnp.mean(x * x, axis=-1, keepdims=True) + EPS) * g


def _dot(a, b):
    return jnp.dot(a, b, preferred_element_type=F32)


def _dot_nt(a, b):
    return lax.dot_general(a, b, (((1,), (1,)), ((), ())), preferred_element_type=F32)


def _head_lanes(h):
    return slice((h // 2) * LANES, (h // 2 + 1) * LANES)


def _head_rows(h):
    return slice(h * HEAD_DIM, (h + 1) * HEAD_DIM)


def _isolate_head(pair, h):
    lane = lax.broadcasted_iota(jnp.int32, pair.shape, 1)
    keep = (lane < HEAD_DIM) if h % 2 == 0 else (lane >= HEAD_DIM)
    return jnp.where(keep, pair, jnp.zeros((), pair.dtype))


def _mem_kv_kernel(mem_ref, g_ref, w_ref, km_ref, vmt_ref):
    mn = _rms(mem_ref[0], g_ref[...]).astype(BF16)
    kv = _dot(mn, w_ref[...])
    km_ref[0] = kv[:, :D_MEM].astype(BF16)
    vmt_ref[0] = kv[:, D_MEM:].T.astype(BF16)


def _mem_kv(mem, g, w):
    b, m, d = mem.shape
    return pl.pallas_call(
        _mem_kv_kernel,
        grid=(b,),
        in_specs=[pl.BlockSpec((1, m, d), lambda i: (i, 0, 0)),
                  pl.BlockSpec((1, d), lambda i: (0, 0)),
                  pl.BlockSpec((d, 2 * D_MEM), lambda i: (0, 0))],
        out_specs=[pl.BlockSpec((1, m, D_MEM), lambda i: (i, 0, 0)),
                   pl.BlockSpec((1, D_MEM, m), lambda i: (i, 0, 0))],
        out_shape=[jax.ShapeDtypeStruct((b, m, D_MEM), BF16),
                   jax.ShapeDtypeStruct((b, D_MEM, m), BF16)],
        compiler_params=_params(1),
        name="mem_kv",
    )(mem, g, w)


def _gmlp_chunk(u, v, lng, lnb, w_ref, sb, avg, bg):
    lane = lax.broadcasted_iota(jnp.int32, (GMLP_CHUNK, LANES), 1)
    first_head = lane < HEAD_DIM
    uc = jax.nn.gelu(u)
    vc = jax.nn.gelu(v)
    mu = _dot(vc.astype(BF16), avg)
    dv = vc - mu
    var = _dot((dv * dv).astype(BF16), avg)
    vn = (dv * lax.rsqrt(var + EPS) * lng + lnb).astype(BF16)
    parts = []
    for p in range(GMLP_GROUPS // 2):
        vp = vn[:, p * LANES:(p + 1) * LANES]
        parts.append(jnp.where(first_head, _dot(w_ref[2 * p], vp), _dot(w_ref[2 * p + 1], vp)))
    sv = jnp.concatenate(parts, axis=-1) + sb
    return _rms(uc * sv, bg).astype(BF16)


def _front_kernel(x_ref, g_ref, w_ref, lng_ref, lnb_ref, ws_ref, sb_ref, avg_ref, bga_ref, bgm_ref,
                  km_ref, vmt_ref, ya_ref, q_ref, k_ref, vt_ref, ym_ref, z_ref,
                  u_scr, v_scr, vtmp_ref, omt_ref, *, tm):
    h = _rms(x_ref[0], g_ref[...]).astype(BF16)

    def seg(i, lo=0, hi=None):
        a = _OFF[i] + lo
        b = _OFF[i + 1] if hi is None else _OFF[i] + hi
        return _dot(h, w_ref[:, a:b])

    def gmlp_chunk(c):
        rows = slice(c * GMLP_CHUNK, (c + 1) * GMLP_CHUNK)
        ya_ref[0, rows, :] = _gmlp_chunk(u_scr[rows, :], v_scr[rows, :], lng_ref[...], lnb_ref[...],
                                         ws_ref, sb_ref[...], avg_ref[...], bga_ref[...])

    scale = HEAD_DIM ** -0.5 * LOG2E
    n_chunks = tm // GMLP_CHUNK
    u_scr[...] = seg(0)
    v_scr[...] = seg(1)
    qm = (seg(5) * scale).astype(BF16)
    q_ref[0] = (seg(2) * scale).astype(BF16)
    k_ref[0] = seg(3).astype(BF16)
    for c in range(n_chunks // 2):
        gmlp_chunk(c)
    vtmp_ref[...] = seg(4)
    vt_ref[0] = vtmp_ref[...].T.astype(BF16)
    half = D_MIX // 2
    z_ref[0, :, :half] = seg(6, 0, half).astype(BF16)
    for c in range(n_chunks // 2, n_chunks):
        gmlp_chunk(c)
    z_ref[0, :, half:] = seg(6, half, D_MIX).astype(BF16)

    for qt in range(tm // MOBA_BLOCK):
        rows = slice(qt * MOBA_BLOCK, (qt + 1) * MOBA_BLOCK)
        s_mem = []
        for hh in range(MEM_HEADS):
            q_h = _isolate_head(qm[rows, _head_lanes(hh)], hh)
            s_mem.append(_dot_nt(km_ref[0, :, _head_lanes(hh)], q_h))
        for hh in range(MEM_HEADS):
            s = s_mem[hh]
            p = jnp.exp2(s - jnp.max(s, axis=0, keepdims=True))
            o = _dot(vmt_ref[0, _head_rows(hh), :], p.astype(BF16))
            omt_ref[_head_rows(hh), :] = o * (1.0 / jnp.sum(p, axis=0, keepdims=True))
        ym_ref[0, rows, :] = _rms(omt_ref[...].T, bgm_ref[...]).astype(BF16)


def _front(x, g, w, ln_g, ln_b, w_causal, sb_exp, avg, bga, bgm, km, vmt, tm):
    b, s, d = x.shape
    m = km.shape[1]
    row = lambda n: pl.BlockSpec((1, tm, n), lambda i, j: (i, j, 0))
    full = lambda a: pl.BlockSpec(a.shape, lambda i, j: (0,) * a.ndim)
    sds = lambda *shape: jax.ShapeDtypeStruct(shape, BF16)
    return pl.pallas_call(
        functools.partial(_front_kernel, tm=tm),
        grid=(b, s // tm),
        in_specs=[row(d), full(g), full(w), full(ln_g), full(ln_b), full(w_causal), full(sb_exp),
                  full(avg), full(bga), full(bgm),
                  pl.BlockSpec((1, m, D_MEM), lambda i, j: (i, 0, 0)),
                  pl.BlockSpec((1, D_MEM, m), lambda i, j: (i, 0, 0))],
        out_specs=[row(D_GMLP), row(D_MOBA), row(D_MOBA),
                   pl.BlockSpec((1, D_MOBA, tm), lambda i, j: (i, 0, j)),
                   row(D_MEM), row(D_MIX)],
        out_shape=[sds(b, s, D_GMLP), sds(b, s, D_MOBA), sds(b, s, D_MOBA),
                   sds(b, D_MOBA, s), sds(b, s, D_MEM), sds(b, s, D_MIX)],
        scratch_shapes=[pltpu.VMEM((tm, D_GMLP), F32),
                        pltpu.VMEM((tm, D_GMLP), F32),
                        pltpu.VMEM((tm, D_MOBA), F32),
                        pltpu.VMEM((D_MEM, MOBA_BLOCK), F32)],
        compiler_params=_params(2),
        name="front",
    )(x, g, w, ln_g, ln_b, w_causal, sb_exp, avg, bga, bgm, km, vmt)


def _attn_kernel(q_ref, k_ref, vt_ref, bias_ref, cfar_ref, bgb_ref, yb_ref,
                 kmean_hi, kmean_lo, qh_ref, sel_ref, m_ref, l_ref, ot_ref, sa_ref, sb_ref, *, n_blocks):
    t = pl.program_id(1)
    blk = MOBA_BLOCK

    @pl.when(t == 0)
    def _():
        kmean_hi[...] = jnp.zeros_like(kmean_hi)
        kmean_lo[...] = jnp.zeros_like(kmean_lo)
        for j in range(n_blocks):
            km = jnp.mean(k_ref[0, j * blk:(j + 1) * blk, :].astype(F32), axis=0, keepdims=True)
            hi = km.astype(BF16)
            kmean_hi[j:j + 1, :] = hi
            kmean_lo[j:j + 1, :] = (km - hi.astype(F32)).astype(BF16)

    row_id = lax.broadcasted_iota(jnp.int32, (SEL_ROWS, blk), 0)
    kpos = lax.broadcasted_iota(jnp.int32, (blk, blk), 0)
    qpos = lax.broadcasted_iota(jnp.int32, (blk, blk), 1)
    causal_t = kpos <= qpos

    def qk(j, h):
        j0 = pl.multiple_of(j * blk, blk)
        return _dot_nt(k_ref[0, pl.ds(j0, blk), _head_lanes(h)], qh_ref[h])

    def update(h, s, shift, j):
        j0 = pl.multiple_of(j * blk, blk)
        m_old = m_ref[h]
        m_new = jnp.maximum(m_old, jnp.max(s, axis=0, keepdims=True) + shift)
        alpha = jnp.exp2(m_old - m_new)
        p = jnp.exp2(s - (m_new - shift))
        m_ref[h] = m_new
        l_ref[h] = alpha * l_ref[h] + jnp.sum(p, axis=0, keepdims=True)
        ot_ref[_head_rows(h), :] = alpha * ot_ref[_head_rows(h), :] + _dot(
            vt_ref[0, _head_rows(h), pl.ds(j0, blk)], p.astype(BF16))

    for h in range(MOBA_HEADS):
        q_h = _isolate_head(q_ref[0, :, _head_lanes(h)], h)
        qh_ref[h] = q_h
        gs = _dot_nt(kmean_hi[:, _head_lanes(h)], q_h) + _dot_nt(kmean_lo[:, _head_lanes(h)], q_h)
        gs = jnp.where(row_id < t, gs, NEG)
        rank = jnp.zeros((SEL_ROWS, blk), jnp.int32)
        for i in range(n_blocks):
            gi = gs[i:i + 1, :]
            ahead = (gi > gs) | ((gi == gs) & (row_id > i))
            rank = rank + ahead.astype(jnp.int32)
        sel_ref[h] = ((rank < MOBA_TOPK) & (row_id < t)).astype(F32)
        sb_ref[h] = qk(t, h)

    for h in range(MOBA_HEADS):
        sa_ref[h] = qk(0, h)
        s = jnp.where(causal_t, sb_ref[h] + bias_ref[h, 0], NEG)
        m0 = jnp.max(s, axis=0, keepdims=True)
        p = jnp.exp2(s - m0)
        m_ref[h] = m0
        l_ref[h] = jnp.sum(p, axis=0, keepdims=True)
        t0 = pl.multiple_of(t * blk, blk)
        ot_ref[_head_rows(h), :] = _dot(vt_ref[0, _head_rows(h), pl.ds(t0, blk)], p.astype(BF16))

    def far_step(j, cur_ref, nxt_ref):
        for h in range(MOBA_HEADS):
            nxt_ref[h] = qk(j + 1, h)
            s = jnp.where(sel_ref[h, pl.ds(j, 1), :] > 0.5, cur_ref[h], NEG)
            update(h, s, cfar_ref[h], j)

    def far(j, carry):
        @pl.when(j % 2 == 0)
        def _():
            far_step(j, sa_ref, sb_ref)

        @pl.when(j % 2 == 1)
        def _():
            far_step(j, sb_ref, sa_ref)

        return carry

    lax.fori_loop(0, t - 1, far, 0)

    def prev_step(cur_ref):
        j = t - 1
        for h in range(MOBA_HEADS):
            s = jnp.where(sel_ref[h, pl.ds(j, 1), :] > 0.5, cur_ref[h] + bias_ref[h, 1], NEG)
            update(h, s, 0.0, j)

    @pl.when((t >= 1) & (t % 2 == 1))
    def _():
        prev_step(sa_ref)

    @pl.when((t >= 1) & (t % 2 == 0))
    def _():
        prev_step(sb_ref)

    for h in range(MOBA_HEADS):
        ot_ref[_head_rows(h), :] = ot_ref[_head_rows(h), :] * (1.0 / l_ref[h])
    yb_ref[0] = _rms(ot_ref[...].T, bgb_ref[...]).astype(BF16)


def _attention(q, k, vt, bias, cfar, bgb):
    b, s, _ = q.shape
    nb = s // MOBA_BLOCK
    blk = MOBA_BLOCK
    full = lambda a: pl.BlockSpec(a.shape, lambda i, j: (0,) * a.ndim)
    return pl.pallas_call(
        functools.partial(_attn_kernel, n_blocks=nb),
        grid=(b, nb),
        in_specs=[pl.BlockSpec((1, blk, D_MOBA), lambda i, j: (i, j, 0)),
                  pl.BlockSpec((1, s, D_MOBA), lambda i, j: (i, 0, 0)),
                  pl.BlockSpec((1, D_MOBA, s), lambda i, j: (i, 0, 0)),
                  full(bias), full(cfar), full(bgb)],
        out_specs=pl.BlockSpec((1, blk, D_MOBA), lambda i, j: (i, j, 0)),
        out_shape=jax.ShapeDtypeStruct((b, s, D_MOBA), BF16),
        scratch_shapes=[pltpu.VMEM((SEL_ROWS, D_MOBA), BF16),
                        pltpu.VMEM((SEL_ROWS, D_MOBA), BF16),
                        pltpu.VMEM((MOBA_HEADS, blk, LANES), BF16),
                        pltpu.VMEM((MOBA_HEADS, SEL_ROWS, blk), F32),
                        pltpu.VMEM((MOBA_HEADS, 1, blk), F32),
                        pltpu.VMEM((MOBA_HEADS, 1, blk), F32),
                        pltpu.VMEM((D_MOBA, blk), F32),
                        pltpu.VMEM((MOBA_HEADS, blk, blk), F32),
                        pltpu.VMEM((MOBA_HEADS, blk, blk), F32)],
        compiler_params=_params(2),
        name="attention",
    )(q, k, vt, bias, cfar, bgb)


def _out_proj_kernel(ya_ref, yb_ref, ym_ref, z_ref, x_ref, w_ref, fg_ref, o_ref, *, final):
    y = jnp.concatenate([ya_ref[...], yb_ref[...], ym_ref[...]], axis=-1).astype(F32)
    y = (y * jax.nn.silu(z_ref[...].astype(F32))).astype(BF16)
    out = x_ref[...] + _dot(y, w_ref[...])
    if final:
        out = _rms(out, fg_ref[...])
    o_ref[...] = out


def _out_proj(ya, yb, ym, z, x, w, fg, tm, final):
    t, d = x.shape
    row = lambda n: pl.BlockSpec((tm, n), lambda i: (i, 0))
    return pl.pallas_call(
        functools.partial(_out_proj_kernel, final=final),
        grid=(t // tm,),
        in_specs=[row(D_GMLP), row(D_MOBA), row(D_MEM), row(D_MIX), row(d),
                  pl.BlockSpec(w.shape, lambda i: (0, 0)),
                  pl.BlockSpec(fg.shape, lambda i: (0, 0))],
        out_specs=row(d),
        out_shape=jax.ShapeDtypeStruct((t, d), F32),
        compiler_params=_params(1),
        name="out_proj",
    )(ya, yb, ym, z, x, w, fg)


def _t5_bucket(dist):
    dist = jnp.maximum(dist, 0)
    max_exact = REL_BUCKETS // 2
    df = jnp.maximum(dist, 1).astype(F32)
    large = max_exact + (jnp.log(df / max_exact) / math.log(REL_MAX_DIST / max_exact)
                         * (REL_BUCKETS - max_exact)).astype(jnp.int32)
    large = jnp.minimum(large, REL_BUCKETS - 1)
    return jnp.where(dist < max_exact, dist, large)


def _bias_tables(rel_bias):
    assert MOBA_BLOCK + 1 >= REL_MAX_DIST
    kq = jnp.arange(MOBA_BLOCK)
    d_own = kq[None, :] - kq[:, None]
    buckets = jnp.stack([_t5_bucket(d_own), _t5_bucket(d_own + MOBA_BLOCK)])
    onehot = (buckets[None] == jnp.arange(REL_BUCKETS)[:, None, None, None]).astype(F32)
    table = rel_bias.astype(F32) * LOG2E
    tiles = jnp.einsum('nh,nxkq->hxkq', table, onehot, precision=lax.Precision.HIGHEST)
    cfar = jnp.broadcast_to(table[REL_BUCKETS - 1][:, None, None], (MOBA_HEADS, 1, MOBA_BLOCK))
    return tiles, cfar


def kernel(x, mem, norm_g, w_in, gmlp_ln_g, gmlp_ln_b, spatial_w, spatial_b, rel_bias,
           mem_norm_g, w_mem_kv, branch_norm_g, w_out, final_norm_g):
    b, s, d = x.shape
    depth = w_in.shape[0]
    t = b * s
    tm = 512

    bias, cfar = _bias_tables(rel_bias)
    causal = jnp.tril(jnp.ones((GMLP_CHUNK, GMLP_CHUNK), F32))
    grp = np.arange(D_GMLP) // HEAD_DIM
    avg = jnp.asarray((grp[:, None] == grp[None, :]).astype(np.float32) / HEAD_DIM, BF16)
    fg = final_norm_g.reshape(1, d)

    for l in range(depth):
        km, vmt = _mem_kv(mem, mem_norm_g[l].reshape(1, d), w_mem_kv[l].astype(BF16))
        bg = branch_norm_g[l]
        sb_exp = jnp.repeat(spatial_b[l].T, HEAD_DIM, axis=1)
        ya, q, k, vt, ym, z = _front(
            x, norm_g[l].reshape(1, d), w_in[l].astype(BF16),
            gmlp_ln_g[l].reshape(1, D_GMLP), gmlp_ln_b[l].reshape(1, D_GMLP),
            (spatial_w[l] * causal[None]).astype(BF16), sb_exp, avg,
            bg[:D_GMLP].reshape(1, D_GMLP), bg[D_GMLP + D_MOBA:].reshape(1, D_MEM), km, vmt, tm)
        yb = _attention(q, k, vt, bias, cfar, bg[D_GMLP:D_GMLP + D_MOBA].reshape(1, D_MOBA))
        x = _out_proj(ya.reshape(t, D_GMLP), yb.reshape(t, D_MOBA), ym.reshape(t, D_MEM),
                      z.reshape(t, D_MIX), x.reshape(t, d), w_out[l].astype(BF16), fg, tm,
                      final=(l == depth - 1)).reshape(b, s, d)
    return x
```

```python
import functools
import math

import numpy as np
import jax
import jax.numpy as jnp
from jax import lax
from jax.experimental import pallas as pl
from jax.experimental.pallas import tpu as pltpu

D_MODEL = 1024
HEAD_DIM = 64
GMLP_GROUPS = 6
MOBA_HEADS = 6
MEM_HEADS = 4
D_GMLP = GMLP_GROUPS * HEAD_DIM
D_MOBA = MOBA_HEADS * HEAD_DIM
D_MEM = MEM_HEADS * HEAD_DIM
D_MIX = D_GMLP + D_MOBA + D_MEM
D_IN = 2 * D_GMLP + 3 * D_MOBA + D_MEM + D_MIX
GMLP_CHUNK = 128
MOBA_BLOCK = 256
MOBA_TOPK = 3
REL_BUCKETS = 32
REL_MAX_DIST = 128
EPS = 1e-6
NEG = -1e30
LOG2E = math.log2(math.e)

LANES = 128
SEL_ROWS = 16
VMEM_LIMIT = 48 * 1024 * 1024

F32 = jnp.float32
BF16 = jnp.bfloat16

_OFF = np.cumsum([0, D_GMLP, D_GMLP, D_MOBA, D_MOBA, D_MOBA, D_MEM, D_MIX])


def _params(n_axes):
    return pltpu.CompilerParams(
        dimension_semantics=("arbitrary",) * n_axes, vmem_limit_bytes=VMEM_LIMIT)


def _rms(x, g):
    return x * lax.rsqrt(jnp.mean(x * x, axis=-1, keepdims=True) + EPS) * g


def _dot(a, b):
    return jnp.dot(a, b, preferred_element_type=F32)


def _dot_nt(a, b):
    return lax.dot_general(a, b, (((1,), (1,)), ((), ())), preferred_element_type=F32)


def _head_lanes(h):
    return slice((h // 2) * LANES, (h // 2 + 1) * LANES)


def _head_rows(h):
    return slice(h * HEAD_DIM, (h + 1) * HEAD_DIM)


def _isolate_head(pair, h):
    lane = lax.broadcasted_iota(jnp.int32, pair.shape, 1)
    keep = (lane < HEAD_DIM) if h % 2 == 0 else (lane >= HEAD_DIM)
    return jnp.where(keep, pair, jnp.zeros((), pair.dtype))


def _mem_kv_kernel(mem_ref, g_ref, w_ref, km_ref, vmt_ref):
    mn = _rms(mem_ref[0], g_ref[...]).astype(BF16)
    kv = _dot(mn, w_ref[...])
    km_ref[0] = kv[:, :D_MEM].astype(BF16)
    vmt_ref[0] = kv[:, D_MEM:].T.astype(BF16)


def _mem_kv(mem, g, w):
    b, m, d = mem.shape
    return pl.pallas_call(
        _mem_kv_kernel,
        grid=(b,),
        in_specs=[pl.BlockSpec((1, m, d), lambda i: (i, 0, 0)),
                  pl.BlockSpec((1, d), lambda i: (0, 0)),
                  pl.BlockSpec((d, 2 * D_MEM), lambda i: (0, 0))],
        out_specs=[pl.BlockSpec((1, m, D_MEM), lambda i: (i, 0, 0)),
                   pl.BlockSpec((1, D_MEM, m), lambda i: (i, 0, 0))],
        out_shape=[jax.ShapeDtypeStruct((b, m, D_MEM), BF16),
                   jax.ShapeDtypeStruct((b, D_MEM, m), BF16)],
        compiler_params=_params(1),
        name="mem_kv",
    )(mem, g, w)


def _gmlp_chunk(u, v, lng, lnb, w_ref, sb, avg, bg):
    lane = lax.broadcasted_iota(jnp.int32, (GMLP_CHUNK, LANES), 1)
    first_head = lane < HEAD_DIM
    uc = jax.nn.gelu(u)
    vc = jax.nn.gelu(v)
    mu = _dot(vc.astype(BF16), avg)
    dv = vc - mu
    var = _dot((dv * dv).astype(BF16), avg)
    vn = (dv * lax.rsqrt(var + EPS) * lng + lnb).astype(BF16)
    parts = []
    for p in range(GMLP_GROUPS // 2):
        vp = vn[:, p * LANES:(p + 1) * LANES]
        parts.append(jnp.where(first_head, _dot(w_ref[2 * p], vp), _dot(w_ref[2 * p + 1], vp)))
    sv = jnp.concatenate(parts, axis=-1) + sb
    return _rms(uc * sv, bg).astype(BF16)


def _front_kernel(x_ref, g_ref, w_ref, lng_ref, lnb_ref, ws_ref, sb_ref, avg_ref, bga_ref, bgm_ref,
                  km_ref, vmt_ref, ya_ref, q_ref, k_ref, vt_ref, ym_ref, z_ref,
                  u_scr, v_scr, vtmp_ref, omt_ref, *, tm):
    h = _rms(x_ref[0], g_ref[...]).astype(BF16)

    def seg(i, lo=0, hi=None):
        a = _OFF[i] + lo
        b = _OFF[i + 1] if hi is None else _OFF[i] + hi
        return _dot(h, w_ref[:, a:b])

    def gmlp_chunk(c):
        rows = slice(c * GMLP_CHUNK, (c + 1) * GMLP_CHUNK)
        ya_ref[0, rows, :] = _gmlp_chunk(u_scr[rows, :], v_scr[rows, :], lng_ref[...], lnb_ref[...],
                                         ws_ref, sb_ref[...], avg_ref[...], bga_ref[...])

    scale = HEAD_DIM ** -0.5 * LOG2E
    n_chunks = tm // GMLP_CHUNK
    u_scr[...] = seg(0)
    v_scr[...] = seg(1)
    qm = (seg(5) * scale).astype(BF16)
    q_ref[0] = (seg(2) * scale).astype(BF16)
    k_ref[0] = seg(3).astype(BF16)
    for c in range(n_chunks // 2):
        gmlp_chunk(c)
    vtmp_ref[...] = seg(4)
    vt_ref[0] = vtmp_ref[...].T.astype(BF16)
    half = D_MIX // 2
    z_ref[0, :, :half] = seg(6, 0, half).astype(BF16)
    for c in range(n_chunks // 2, n_chunks):
        gmlp_chunk(c)
    z_ref[0, :, half:] = seg(6, half, D_MIX).astype(BF16)

    for qt in range(tm // MOBA_BLOCK):
        rows = slice(qt * MOBA_BLOCK, (qt + 1) * MOBA_BLOCK)
        s_mem = []
        for hh in range(MEM_HEADS):
            q_h = _isolate_head(qm[rows, _head_lanes(hh)], hh)
            s_mem.append(_dot_nt(km_ref[0, :, _head_lanes(hh)], q_h))
        for hh in range(MEM_HEADS):
            s = s_mem[hh]
            p = jnp.exp2(s - jnp.max(s, axis=0, keepdims=True))
            o = _dot(vmt_ref[0, _head_rows(hh), :], p.astype(BF16))
            omt_ref[_head_rows(hh), :] = o * (1.0 / jnp.sum(p, axis=0, keepdims=True))
        ym_ref[0, rows, :] = _rms(omt_ref[...].T, bgm_ref[...]).astype(BF16)


def _front(x, g, w, ln_g, ln_b, w_causal, sb_exp, avg, bga, bgm, km, vmt, tm):
    b, s, d = x.shape
    m = km.shape[1]
    row = lambda n: pl.BlockSpec((1, tm, n), lambda i, j: (i, j, 0))
    full = lambda a: pl.BlockSpec(a.shape, lambda i, j: (0,) * a.ndim)
    sds = lambda *shape: jax.ShapeDtypeStruct(shape, BF16)
    return pl.pallas_call(
        functools.partial(_front_kernel, tm=tm),
        grid=(b, s // tm),
        in_specs=[row(d), full(g), full(w), full(ln_g), full(ln_b), full(w_causal), full(sb_exp),
                  full(avg), full(bga), full(bgm),
                  pl.BlockSpec((1, m, D_MEM), lambda i, j: (i, 0, 0)),
                  pl.BlockSpec((1, D_MEM, m), lambda i, j: (i, 0, 0))],
        out_specs=[row(D_GMLP), row(D_MOBA), row(D_MOBA),
                   pl.BlockSpec((1, D_MOBA, tm), lambda i, j: (i, 0, j)),
                   row(D_MEM), row(D_MIX)],
        out_shape=[sds(b, s, D_GMLP), sds(b, s, D_MOBA), sds(b, s, D_MOBA),
                   sds(b, D_MOBA, s), sds(b, s, D_MEM), sds(b, s, D_MIX)],
        scratch_shapes=[pltpu.VMEM((tm, D_GMLP), F32),
                        pltpu.VMEM((tm, D_GMLP), F32),
                        pltpu.VMEM((tm, D_MOBA), F32),
                        pltpu.VMEM((D_MEM, MOBA_BLOCK), F32)],
        compiler_params=_params(2),
        name="front",
    )(x, g, w, ln_g, ln_b, w_causal, sb_exp, avg, bga, bgm, km, vmt)


def _attn_kernel(q_ref, k_ref, vt_ref, bias_ref, cfar_ref, bgb_ref,
                 ya_ref, ym_ref, z_ref, x_ref, wo_ref, fg_ref, o_ref,
                 kmean_hi, kmean_lo, qh_ref, sel_ref, m_ref, l_ref, ot_ref, sa_ref, sb_ref,
                 *, n_blocks, final):
    t = pl.program_id(1)
    blk = MOBA_BLOCK

    @pl.when(t == 0)
    def _():
        kmean_hi[...] = jnp.zeros_like(kmean_hi)
        kmean_lo[...] = jnp.zeros_like(kmean_lo)
        for j in range(n_blocks):
            km = jnp.mean(k_ref[0, j * blk:(j + 1) * blk, :].astype(F32), axis=0, keepdims=True)
            hi = km.astype(BF16)
            kmean_hi[j:j + 1, :] = hi
            kmean_lo[j:j + 1, :] = (km - hi.astype(F32)).astype(BF16)

    row_id = lax.broadcasted_iota(jnp.int32, (SEL_ROWS, blk), 0)
    kpos = lax.broadcasted_iota(jnp.int32, (blk, blk), 0)
    qpos = lax.broadcasted_iota(jnp.int32, (blk, blk), 1)
    causal_t = kpos <= qpos

    def qk(j, h):
        j0 = pl.multiple_of(j * blk, blk)
        return _dot_nt(k_ref[0, pl.ds(j0, blk), _head_lanes(h)], qh_ref[h])

    def update(h, s, shift, j):
        j0 = pl.multiple_of(j * blk, blk)
        m_old = m_ref[h]
        m_new = jnp.maximum(m_old, jnp.max(s, axis=0, keepdims=True) + shift)
        alpha = jnp.exp2(m_old - m_new)
        p = jnp.exp2(s - (m_new - shift))
        m_ref[h] = m_new
        l_ref[h] = alpha * l_ref[h] + jnp.sum(p, axis=0, keepdims=True)
        ot_ref[_head_rows(h), :] = alpha * ot_ref[_head_rows(h), :] + _dot(
            vt_ref[0, _head_rows(h), pl.ds(j0, blk)], p.astype(BF16))

    for h in range(MOBA_HEADS):
        q_h = _isolate_head(q_ref[0, :, _head_lanes(h)], h)
        qh_ref[h] = q_h
        gs = _dot_nt(kmean_hi[:, _head_lanes(h)], q_h) + _dot_nt(kmean_lo[:, _head_lanes(h)], q_h)
        gs = jnp.where(row_id < t, gs, NEG)
        rank = jnp.zeros((SEL_ROWS, blk), jnp.int32)
        for i in range(n_blocks):
            gi = gs[i:i + 1, :]
            ahead = (gi > gs) | ((gi == gs) & (row_id > i))
            rank = rank + ahead.astype(jnp.int32)
        sel_ref[h] = ((rank < MOBA_TOPK) & (row_id < t)).astype(F32)
        sb_ref[h] = qk(t, h)

    for h in range(MOBA_HEADS):
        sa_ref[h] = qk(0, h)
        s = jnp.where(causal_t, sb_ref[h] + bias_ref[h, 0], NEG)
        m0 = jnp.max(s, axis=0, keepdims=True)
        p = jnp.exp2(s - m0)
        m_ref[h] = m0
        l_ref[h] = jnp.sum(p, axis=0, keepdims=True)
        t0 = pl.multiple_of(t * blk, blk)
        ot_ref[_head_rows(h), :] = _dot(vt_ref[0, _head_rows(h), pl.ds(t0, blk)], p.astype(BF16))

    def far_step(j, cur_ref, nxt_ref):
        for h in range(MOBA_HEADS):
            nxt_ref[h] = qk(j + 1, h)
            s = jnp.where(sel_ref[h, pl.ds(j, 1), :] > 0.5, cur_ref[h], NEG)
            update(h, s, cfar_ref[h], j)

    def far(j, carry):
        @pl.when(j % 2 == 0)
        def _():
            far_step(j, sa_ref, sb_ref)

        @pl.when(j % 2 == 1)
        def _():
            far_step(j, sb_ref, sa_ref)

        return carry

    lax.fori_loop(0, t - 1, far, 0)

    def prev_step(cur_ref):
        j = t - 1
        for h in range(MOBA_HEADS):
            s = jnp.where(sel_ref[h, pl.ds(j, 1), :] > 0.5, cur_ref[h] + bias_ref[h, 1], NEG)
            update(h, s, 0.0, j)

    @pl.when((t >= 1) & (t % 2 == 1))
    def _():
        prev_step(sa_ref)

    @pl.when((t >= 1) & (t % 2 == 0))
    def _():
        prev_step(sb_ref)

    for h in range(MOBA_HEADS):
        ot_ref[_head_rows(h), :] = ot_ref[_head_rows(h), :] * (1.0 / l_ref[h])
    yb = _rms(ot_ref[...].T, bgb_ref[...]).astype(BF16)

    y = jnp.concatenate([ya_ref[0], yb, ym_ref[0]], axis=-1).astype(F32)
    y = (y * jax.nn.silu(z_ref[0].astype(F32))).astype(BF16)
    out = x_ref[0] + _dot(y, wo_ref[...])
    if final:
        out = _rms(out, fg_ref[...])
    o_ref[0] = out


def _attention(q, k, vt, bias, cfar, bgb, ya, ym, z, x, w_out, fg, final):
    b, s, d = x.shape
    nb = s // MOBA_BLOCK
    blk = MOBA_BLOCK
    full = lambda a: pl.BlockSpec(a.shape, lambda i, j: (0,) * a.ndim)
    row = lambda n: pl.BlockSpec((1, blk, n), lambda i, j: (i, j, 0))
    return pl.pallas_call(
        functools.partial(_attn_kernel, n_blocks=nb, final=final),
        grid=(b, nb),
        in_specs=[row(D_MOBA),
                  pl.BlockSpec((1, s, D_MOBA), lambda i, j: (i, 0, 0)),
                  pl.BlockSpec((1, D_MOBA, s), lambda i, j: (i, 0, 0)),
                  full(bias), full(cfar), full(bgb),
                  row(D_GMLP), row(D_MEM), row(D_MIX), row(d), full(w_out), full(fg)],
        out_specs=row(d),
        out_shape=jax.ShapeDtypeStruct((b, s, d), F32),
        scratch_shapes=[pltpu.VMEM((SEL_ROWS, D_MOBA), BF16),
                        pltpu.VMEM((SEL_ROWS, D_MOBA), BF16),
                        pltpu.VMEM((MOBA_HEADS, blk, LANES), BF16),
                        pltpu.VMEM((MOBA_HEADS, SEL_ROWS, blk), F32),
                        pltpu.VMEM((MOBA_HEADS, 1, blk), F32),
                        pltpu.VMEM((MOBA_HEADS, 1, blk), F32),
                        pltpu.VMEM((D_MOBA, blk), F32),
                        pltpu.VMEM((MOBA_HEADS, blk, blk), F32),
                        pltpu.VMEM((MOBA_HEADS, blk, blk), F32)],
        compiler_params=_params(2),
        name="attention",
    )(q, k, vt, bias, cfar, bgb, ya, ym, z, x, w_out, fg)


def _t5_bucket(dist):
    dist = jnp.maximum(dist, 0)
    max_exact = REL_BUCKETS // 2
    df = jnp.maximum(dist, 1).astype(F32)
    large = max_exact + (jnp.log(df / max_exact) / math.log(REL_MAX_DIST / max_exact)
                         * (REL_BUCKETS - max_exact)).astype(jnp.int32)
    large = jnp.minimum(large, REL_BUCKETS - 1)
    return jnp.where(dist < max_exact, dist, large)


def _bias_tables(rel_bias):
    assert MOBA_BLOCK + 1 >= REL_MAX_DIST
    kq = jnp.arange(MOBA_BLOCK)
    d_own = kq[None, :] - kq[:, None]
    buckets = jnp.stack([_t5_bucket(d_own), _t5_bucket(d_own + MOBA_BLOCK)])
    onehot = (buckets[None] == jnp.arange(REL_BUCKETS)[:, None, None, None]).astype(F32)
    table = rel_bias.astype(F32) * LOG2E
    tiles = jnp.einsum('nh,nxkq->hxkq', table, onehot, precision=lax.Precision.HIGHEST)
    cfar = jnp.broadcast_to(table[REL_BUCKETS - 1][:, None, None], (MOBA_HEADS, 1, MOBA_BLOCK))
    return tiles, cfar


def kernel(x, mem, norm_g, w_in, gmlp_ln_g, gmlp_ln_b, spatial_w, spatial_b, rel_bias,
           mem_norm_g, w_mem_kv, branch_norm_g, w_out, final_norm_g):
    b, s, d = x.shape
    depth = w_in.shape[0]
    tm = 512

    bias, cfar = _bias_tables(rel_bias)
    causal = jnp.tril(jnp.ones((GMLP_CHUNK, GMLP_CHUNK), F32))
    grp = np.arange(D_GMLP) // HEAD_DIM
    avg = jnp.asarray((grp[:, None] == grp[None, :]).astype(np.float32) / HEAD_DIM, BF16)
    fg = final_norm_g.reshape(1, d)

    for l in range(depth):
        km, vmt = _mem_kv(mem, mem_norm_g[l].reshape(1, d), w_mem_kv[l].astype(BF16))
        bg = branch_norm_g[l]
        sb_exp = jnp.repeat(spatial_b[l].T, HEAD_DIM, axis=1)
        ya, q, k, vt, ym, z = _front(
            x, norm_g[l].reshape(1, d), w_in[l].astype(BF16),
            gmlp_ln_g[l].reshape(1, D_GMLP), gmlp_ln_b[l].reshape(1, D_GMLP),
            (spatial_w[l] * causal[None]).astype(BF16), sb_exp, avg,
            bg[:D_GMLP].reshape(1, D_GMLP), bg[D_GMLP + D_MOBA:].reshape(1, D_MEM), km, vmt, tm)
        x = _attention(q, k, vt, bias, cfar, bg[D_GMLP:D_GMLP + D_MOBA].reshape(1, D_MOBA),
                       ya, ym, z, x, w_out[l].astype(BF16), fg, final=(l == depth - 1))
    return x
```

```python
import functools
import math

import numpy as np
import jax
import jax.numpy as jnp
from jax import lax
from jax.experimental import pallas as pl
from jax.experimental.pallas import tpu as pltpu

D_MODEL = 1024
HEAD_DIM = 64
GMLP_GROUPS = 6
MOBA_HEADS = 6
MEM_HEADS = 4
D_GMLP = GMLP_GROUPS * HEAD_DIM
D_MOBA = MOBA_HEADS * HEAD_DIM
D_MEM = MEM_HEADS * HEAD_DIM
D_MIX = D_GMLP + D_MOBA + D_MEM
D_IN = 2 * D_GMLP + 3 * D_MOBA + D_MEM + D_MIX
GMLP_CHUNK = 128
MOBA_BLOCK = 256
MOBA_TOPK = 3
REL_BUCKETS = 32
REL_MAX_DIST = 128
EPS = 1e-6
NEG = -1e30
LOG2E = math.log2(math.e)

LANES = 128
SEL_ROWS = 16
VMEM_LIMIT = 56 * 1024 * 1024

F32 = jnp.float32
BF16 = jnp.bfloat16

_OFF = np.cumsum([0, D_GMLP, D_GMLP, D_MOBA, D_MOBA, D_MOBA, D_MEM, D_MIX])


def _params(n_axes):
    return pltpu.CompilerParams(
        dimension_semantics=("arbitrary",) * n_axes, vmem_limit_bytes=VMEM_LIMIT)


def _rms(x, g):
    return x * lax.rsqrt(jnp.mean(x * x, axis=-1, keepdims=True) + EPS) * g


def _dot(a, b):
    return jnp.dot(a, b, preferred_element_type=F32)


def _dot_nt(a, b):
    return lax.dot_general(a, b, (((1,), (1,)), ((), ())), preferred_element_type=F32)


def _head_lanes(h):
    return slice((h // 2) * LANES, (h // 2 + 1) * LANES)


def _head_rows(h):
    return slice(h * HEAD_DIM, (h + 1) * HEAD_DIM)


def _isolate_head(pair, h):
    lane = lax.broadcasted_iota(jnp.int32, pair.shape, 1)
    keep = (lane < HEAD_DIM) if h % 2 == 0 else (lane >= HEAD_DIM)
    return jnp.where(keep, pair, jnp.zeros((), pair.dtype))


def _mem_kv_kernel(mem_ref, g_ref, w_ref, km_ref, vmt_ref):
    mn = _rms(mem_ref[0], g_ref[...]).astype(BF16)
    kv = _dot(mn, w_ref[...])
    km_ref[0] = kv[:, :D_MEM].astype(BF16)
    vmt_ref[0] = kv[:, D_MEM:].T.astype(BF16)


def _mem_kv(mem, g, w):
    b, m, d = mem.shape
    return pl.pallas_call(
        _mem_kv_kernel,
        grid=(b,),
        in_specs=[pl.BlockSpec((1, m, d), lambda i: (i, 0, 0)),
                  pl.BlockSpec((1, d), lambda i: (0, 0)),
                  pl.BlockSpec((d, 2 * D_MEM), lambda i: (0, 0))],
        out_specs=[pl.BlockSpec((1, m, D_MEM), lambda i: (i, 0, 0)),
                   pl.BlockSpec((1, D_MEM, m), lambda i: (i, 0, 0))],
        out_shape=[jax.ShapeDtypeStruct((b, m, D_MEM), BF16),
                   jax.ShapeDtypeStruct((b, D_MEM, m), BF16)],
        compiler_params=_params(1),
        name="mem_kv",
    )(mem, g, w)


def _layer_kernel(xf_ref, xa_ref, g_ref, w_ref, lng_ref, lnb_ref, ws_ref, sbx_ref, avg_ref, bga_ref,
                  bgm_ref, km_ref, vmt_ref, bias_ref, cfar_ref, bgb_ref, wo_ref, fg_ref,
                  o_ref,
                  k_scr, vt_scr, kmean_scr,
                  q_cur, ya_cur, ym_cur, z_cur,
                  q_new, ya_new, ym_new, z_new, k_new, vt_new, kmean_new,
                  u_scr, v_scr, vtmp_ref, qm_scr, omt_ref,
                  qh_ref, sel_ref, m_ref, l_ref, ot_ref, sa_ref, sb_ref,
                  *, n_tiles, n_blocks, final):
    g = pl.program_id(0)
    blk = MOBA_BLOCK
    f = jnp.minimum(g, n_tiles - 1)
    a = f
    tf = f % n_blocks
    pf = (f // n_blocks) % 2
    t = a % n_blocks
    pa = (a // n_blocks) % 2

    @pl.when(g == 0)
    def _():
        q_cur[...] = jnp.zeros_like(q_cur)
        ya_cur[...] = jnp.zeros_like(ya_cur)
        ym_cur[...] = jnp.zeros_like(ym_cur)
        z_cur[...] = jnp.zeros_like(z_cur)
        kmean_scr[...] = jnp.zeros_like(kmean_scr)
        k_scr[0] = jnp.zeros((blk, D_MOBA), BF16)
        vt_scr[0] = jnp.zeros((D_MOBA, blk), BF16)

    h_in = _rms(xf_ref[0], g_ref[...]).astype(BF16)
    scale = HEAD_DIM ** -0.5 * LOG2E

    def seg(i, lo=0, hi=None):
        c0 = _OFF[i] + lo
        c1 = _OFF[i + 1] if hi is None else _OFF[i] + hi
        return _dot(h_in, w_ref[:, c0:c1])

    def f_u():
        u_scr[...] = seg(0)

    def f_v():
        v_scr[...] = seg(1)

    def f_qm():
        qm_scr[...] = (seg(5) * scale).astype(BF16)

    def f_q():
        q_new[...] = (seg(2) * scale).astype(BF16)

    def f_k():
        kf = seg(3)
        k_new[...] = kf.astype(BF16)
        kmean_new[...] = jnp.mean(kf, axis=0, keepdims=True)

    def f_vt():
        vtmp_ref[...] = seg(4)
        vt_new[...] = vtmp_ref[...].T.astype(BF16)

    def f_z(i):
        w = D_MIX // 4
        z_new[:, i * w:(i + 1) * w] = seg(6, i * w, (i + 1) * w).astype(BF16)

    n_chunks = blk // GMLP_CHUNK
    gm = {}
    lane_c = lax.broadcasted_iota(jnp.int32, (GMLP_CHUNK, LANES), 1)
    first_head = lane_c < HEAD_DIM

    def chunk_rows(c):
        return slice(c * GMLP_CHUNK, (c + 1) * GMLP_CHUNK)

    def gm_a():
        for c in range(n_chunks):
            vc = jax.nn.gelu(v_scr[chunk_rows(c), :])
            gm[c] = (vc, _dot(vc.astype(BF16), avg_ref[...]))

    def gm_b():
        for c in range(n_chunks):
            vc, mu = gm[c]
            dv = vc - mu
            gm[c] = (dv, _dot((dv * dv).astype(BF16), avg_ref[...]))

    def gm_c():
        for c in range(n_chunks):
            dv, var = gm[c]
            vn = (dv * lax.rsqrt(var + EPS) * lng_ref[...] + lnb_ref[...]).astype(BF16)
            parts = []
            for p in range(GMLP_GROUPS // 2):
                vp = vn[:, p * LANES:(p + 1) * LANES]
                parts.append(jnp.where(first_head, _dot(ws_ref[2 * p], vp), _dot(ws_ref[2 * p + 1], vp)))
            gm[c] = parts

    def gm_d():
        for c in range(n_chunks):
            sv = jnp.concatenate(gm[c], axis=-1) + sbx_ref[...]
            uc = jax.nn.gelu(u_scr[chunk_rows(c), :])
            ya_new[chunk_rows(c), :] = _rms(uc * sv, bga_ref[...]).astype(BF16)

    ma = {}

    def ma_a():
        for hh in range(MEM_HEADS):
            q_h = _isolate_head(qm_scr[:, _head_lanes(hh)], hh)
            ma[hh] = _dot_nt(km_ref[0, :, _head_lanes(hh)], q_h)

    def ma_b():
        for hh in range(MEM_HEADS):
            s = ma[hh]
            p = jnp.exp2(s - jnp.max(s, axis=0, keepdims=True))
            o = _dot(vmt_ref[0, _head_rows(hh), :], p.astype(BF16))
            omt_ref[_head_rows(hh), :] = o * (1.0 / jnp.sum(p, axis=0, keepdims=True))

    def ma_c():
        ym_new[...] = _rms(omt_ref[...].T, bgm_ref[...]).astype(BF16)

    row_id = lax.broadcasted_iota(jnp.int32, (SEL_ROWS, blk), 0)
    kpos = lax.broadcasted_iota(jnp.int32, (blk, blk), 0)
    qpos = lax.broadcasted_iota(jnp.int32, (blk, blk), 1)
    causal_t = kpos <= qpos
    jp = jnp.maximum(t - 1, 0)

    def qk(j, h):
        return _dot_nt(k_scr[pa * n_blocks + j, :, _head_lanes(h)], qh_ref[h])

    def update(h, s, shift, j):
        m_old = m_ref[h]
        m_new = jnp.maximum(m_old, jnp.max(s, axis=0, keepdims=True) + shift)
        alpha = jnp.exp2(m_old - m_new)
        p = jnp.exp2(s - (m_new - shift))
        m_ref[h] = m_new
        l_ref[h] = alpha * l_ref[h] + jnp.sum(p, axis=0, keepdims=True)
        ot_ref[_head_rows(h), :] = alpha * ot_ref[_head_rows(h), :] + _dot(
            vt_scr[pa * n_blocks + j, _head_rows(h), :], p.astype(BF16))

    for _task in (f_u, f_v, gm_a, f_qm, f_q, gm_b, f_k, f_vt, gm_c, gm_d, lambda: f_z(0), lambda: f_z(1), lambda: f_z(2), ma_a, lambda: f_z(3), ma_b, ma_c):
        _task()
    k_scr[pf * n_blocks + tf] = k_new[...]
    vt_scr[pf * n_blocks + tf] = vt_new[...]
    mean_rows = lax.broadcasted_iota(jnp.int32, (SEL_ROWS, D_MOBA), 0)
    kmean_scr[pf] = jnp.where(mean_rows == tf, kmean_new[...], kmean_scr[pf])
    q_cur[...] = q_new[...]
    ya_cur[...] = ya_new[...]
    ym_cur[...] = ym_new[...]
    z_cur[...] = z_new[...]

    km_all = kmean_scr[pa]
    km_hi = km_all.astype(BF16)
    km_lo = (km_all - km_hi.astype(F32)).astype(BF16)
    stage0_tasks = {}
    for h in range(MOBA_HEADS):
        q_h = _isolate_head(q_cur[:, _head_lanes(h)], h)
        qh_ref[h] = q_h
        gs = _dot_nt(km_hi[:, _head_lanes(h)], q_h) + _dot_nt(km_lo[:, _head_lanes(h)], q_h)
        gs = jnp.where(row_id < t, gs, NEG)
        rank = jnp.zeros((SEL_ROWS, blk), jnp.int32)
        for i in range(n_blocks):
            gi = gs[i:i + 1, :]
            ahead = (gi > gs) | ((gi == gs) & (row_id > i))
            rank = rank + ahead.astype(jnp.int32)
        sel_ref[h] = ((rank < MOBA_TOPK) & (row_id < t)).astype(F32)
        sb_ref[h] = qk(t, h)
        if h in stage0_tasks:
            stage0_tasks[h]()

    stage1_tasks = {i: [] for i in range(6)}
    for h in range(MOBA_HEADS):
        sa_ref[h] = qk(jp, h)
        for task in stage1_tasks[h]:
            task()
        s = jnp.where(causal_t, sb_ref[h] + bias_ref[h, 0], NEG)
        m0 = jnp.max(s, axis=0, keepdims=True)
        p = jnp.exp2(s - m0)
        m_ref[h] = m0
        l_ref[h] = jnp.sum(p, axis=0, keepdims=True)
        ot_ref[_head_rows(h), :] = _dot(vt_scr[pa * n_blocks + t, _head_rows(h), :], p.astype(BF16))

    stage2_tasks = {i: [] for i in range(6)}
    for h in range(MOBA_HEADS):
        sb_ref[h] = qk(0, h)
        for task in stage2_tasks[h]:
            task()
        s = jnp.where(sel_ref[h, pl.ds(jp, 1), :] > 0.5, sa_ref[h] + bias_ref[h, 1], NEG)
        update(h, s, 0.0, jp)

    def far_step(j, cur_ref, nxt_ref):
        for h in range(MOBA_HEADS):
            nxt_ref[h] = qk(j + 1, h)
            s = jnp.where(sel_ref[h, pl.ds(j, 1), :] > 0.5, cur_ref[h], NEG)
            update(h, s, cfar_ref[h], j)

    def far(j, carry):
        @pl.when(j % 2 == 0)
        def _():
            far_step(j, sb_ref, sa_ref)

        @pl.when(j % 2 == 1)
        def _():
            far_step(j, sa_ref, sb_ref)

        return carry

    lax.fori_loop(0, t - 1, far, 0)

    for h in range(MOBA_HEADS):
        ot_ref[_head_rows(h), :] = ot_ref[_head_rows(h), :] * (1.0 / l_ref[h])
    yb = _rms(ot_ref[...].T, bgb_ref[...]).astype(BF16)
    y = jnp.concatenate([ya_cur[...], yb, ym_cur[...]], axis=-1).astype(F32)
    y = (y * jax.nn.silu(z_cur[...].astype(F32))).astype(BF16)
    out = xa_ref[0] + _dot(y, wo_ref[...])
    if final:
        out = _rms(out, fg_ref[...])
    o_ref[0] = out


def _layer(x, km, vmt, g_in, w_in, ln_g, ln_b, w_causal, sb_exp, avg, bga, bgm, bias, cfar, bgb,
           w_out, fg, final):
    b, s, d = x.shape
    m = km.shape[1]
    blk = MOBA_BLOCK
    nb = s // blk
    n_tiles = b * nb
    full = lambda arr: pl.BlockSpec(arr.shape, lambda g: (0,) * arr.ndim)

    def front_tile(g):
        f = jnp.minimum(g, n_tiles - 1)
        return f // nb, f % nb

    def back_tile(g):
        a = jnp.minimum(g, n_tiles - 1)
        return a // nb, a % nb

    x_front = pl.BlockSpec((1, blk, d), lambda g: (*front_tile(g), 0))
    x_back = pl.BlockSpec((1, blk, d), lambda g: (*back_tile(g), 0))
    mem_spec = lambda shape: pl.BlockSpec(shape, lambda g: (front_tile(g)[0], 0, 0))
    vm = lambda shape, dt: pltpu.VMEM(shape, dt)
    return pl.pallas_call(
        functools.partial(_layer_kernel, n_tiles=n_tiles, n_blocks=nb, final=final),
        grid=(n_tiles,),
        in_specs=[x_front, x_back, full(g_in), full(w_in), full(ln_g), full(ln_b), full(w_causal),
                  full(sb_exp), full(avg), full(bga), full(bgm),
                  mem_spec((1, m, D_MEM)), mem_spec((1, D_MEM, m)),
                  full(bias), full(cfar), full(bgb), full(w_out), full(fg)],
        out_specs=x_back,
        out_shape=jax.ShapeDtypeStruct((b, s, d), F32),
        scratch_shapes=[
            vm((2 * nb, blk, D_MOBA), BF16),
            vm((2 * nb, D_MOBA, blk), BF16),
            vm((2, SEL_ROWS, D_MOBA), F32),
            vm((blk, D_MOBA), BF16), vm((blk, D_GMLP), BF16), vm((blk, D_MEM), BF16),
            vm((blk, D_MIX), BF16),
            vm((blk, D_MOBA), BF16), vm((blk, D_GMLP), BF16), vm((blk, D_MEM), BF16),
            vm((blk, D_MIX), BF16),
            vm((blk, D_MOBA), BF16), vm((D_MOBA, blk), BF16), vm((1, D_MOBA), F32),
            vm((blk, D_GMLP), F32), vm((blk, D_GMLP), F32),
            vm((blk, D_MOBA), F32),
            vm((blk, D_MEM), BF16),
            vm((D_MEM, blk), F32),
            vm((MOBA_HEADS, blk, LANES), BF16),
            vm((MOBA_HEADS, SEL_ROWS, blk), F32),
            vm((MOBA_HEADS, 1, blk), F32),
            vm((MOBA_HEADS, 1, blk), F32),
            vm((D_MOBA, blk), F32),
            vm((MOBA_HEADS, blk, blk), F32),
            vm((MOBA_HEADS, blk, blk), F32)],
        compiler_params=_params(1),
        name="layer",
    )(x, x, g_in, w_in, ln_g, ln_b, w_causal, sb_exp, avg, bga, bgm, km, vmt, bias, cfar, bgb, w_out, fg)


def _t5_bucket(dist):
    dist = jnp.maximum(dist, 0)
    max_exact = REL_BUCKETS // 2
    df = jnp.maximum(dist, 1).astype(F32)
    large = max_exact + (jnp.log(df / max_exact) / math.log(REL_MAX_DIST / max_exact)
                         * (REL_BUCKETS - max_exact)).astype(jnp.int32)
    large = jnp.minimum(large, REL_BUCKETS - 1)
    return jnp.where(dist < max_exact, dist, large)


def _bias_tables(rel_bias):
    assert MOBA_BLOCK + 1 >= REL_MAX_DIST
    kq = jnp.arange(MOBA_BLOCK)
    d_own = kq[None, :] - kq[:, None]
    buckets = jnp.stack([_t5_bucket(d_own), _t5_bucket(d_own + MOBA_BLOCK)])
    onehot = (buckets[None] == jnp.arange(REL_BUCKETS)[:, None, None, None]).astype(F32)
    table = rel_bias.astype(F32) * LOG2E
    tiles = jnp.einsum('nh,nxkq->hxkq', table, onehot, precision=lax.Precision.HIGHEST)
    cfar = jnp.broadcast_to(table[REL_BUCKETS - 1][:, None, None], (MOBA_HEADS, 1, MOBA_BLOCK))
    return tiles, cfar


def kernel(x, mem, norm_g, w_in, gmlp_ln_g, gmlp_ln_b, spatial_w, spatial_b, rel_bias,
           mem_norm_g, w_mem_kv, branch_norm_g, w_out, final_norm_g):
    b, s, d = x.shape
    depth = w_in.shape[0]

    bias, cfar = _bias_tables(rel_bias)
    causal = jnp.tril(jnp.ones((GMLP_CHUNK, GMLP_CHUNK), F32))
    grp = np.arange(D_GMLP) // HEAD_DIM
    avg = jnp.asarray((grp[:, None] == grp[None, :]).astype(np.float32) / HEAD_DIM, BF16)
    fg = final_norm_g.reshape(1, d)

    for l in range(depth):
        km, vmt = _mem_kv(mem, mem_norm_g[l].reshape(1, d), w_mem_kv[l].astype(BF16))
        bg = branch_norm_g[l]
        sb_exp = jnp.repeat(spatial_b[l].T, HEAD_DIM, axis=1)
        x = _layer(x, km, vmt, norm_g[l].reshape(1, d), w_in[l].astype(BF16),
                   gmlp_ln_g[l].reshape(1, D_GMLP), gmlp_ln_b[l].reshape(1, D_GMLP),
                   (spatial_w[l] * causal[None]).astype(BF16), sb_exp, avg,
                   bg[:D_GMLP].reshape(1, D_GMLP), bg[D_GMLP + D_MOBA:].reshape(1, D_MEM),
                   bias, cfar, bg[D_GMLP:D_GMLP + D_MOBA].reshape(1, D_MOBA),
                   w_out[l].astype(BF16), fg, final=(l == depth - 1))
    return x
```

```python
import functools
import math

import numpy as np
import jax
import jax.numpy as jnp
from jax import lax
from jax.experimental import pallas as pl
from jax.experimental.pallas import tpu as pltpu

D_MODEL = 1024
HEAD_DIM = 64
GMLP_GROUPS = 6
MOBA_HEADS = 6
MEM_HEADS = 4
D_GMLP = GMLP_GROUPS * HEAD_DIM
D_MOBA = MOBA_HEADS * HEAD_DIM
D_MEM = MEM_HEADS * HEAD_DIM
D_MIX = D_GMLP + D_MOBA + D_MEM
D_IN = 2 * D_GMLP + 3 * D_MOBA + D_MEM + D_MIX
GMLP_CHUNK = 128
MOBA_BLOCK = 256
MOBA_TOPK = 3
REL_BUCKETS = 32
REL_MAX_DIST = 128
EPS = 1e-6
NEG = -1e30
LOG2E = math.log2(math.e)

LANES = 128
SEL_ROWS = 16
VMEM_LIMIT = 56 * 1024 * 1024

F32 = jnp.float32
BF16 = jnp.bfloat16

_OFF = np.cumsum([0, D_GMLP, D_GMLP, D_MOBA, D_MOBA, D_MOBA, D_MEM, D_MIX])


def _params(n_axes):
    return pltpu.CompilerParams(
        dimension_semantics=("arbitrary",) * n_axes, vmem_limit_bytes=VMEM_LIMIT)


def _rms(x, g):
    return x * lax.rsqrt(jnp.mean(x * x, axis=-1, keepdims=True) + EPS) * g


def _dot(a, b):
    return jnp.dot(a, b, preferred_element_type=F32)


def _dot_nt(a, b):
    return lax.dot_general(a, b, (((1,), (1,)), ((), ())), preferred_element_type=F32)


def _head_lanes(h):
    return slice((h // 2) * LANES, (h // 2 + 1) * LANES)


def _head_rows(h):
    return slice(h * HEAD_DIM, (h + 1) * HEAD_DIM)


def _isolate_head(pair, h):
    lane = lax.broadcasted_iota(jnp.int32, pair.shape, 1)
    keep = (lane < HEAD_DIM) if h % 2 == 0 else (lane >= HEAD_DIM)
    return jnp.where(keep, pair, jnp.zeros((), pair.dtype))


def _mem_kv_kernel(mem_ref, g_ref, w_ref, km_ref, vmt_ref):
    mn = _rms(mem_ref[0], g_ref[...]).astype(BF16)
    kv = _dot(mn, w_ref[...])
    km_ref[0] = kv[:, :D_MEM].astype(BF16)
    vmt_ref[0] = kv[:, D_MEM:].T.astype(BF16)


def _mem_kv(mem, g, w):
    b, m, d = mem.shape
    return pl.pallas_call(
        _mem_kv_kernel,
        grid=(b,),
        in_specs=[pl.BlockSpec((1, m, d), lambda i: (i, 0, 0)),
                  pl.BlockSpec((1, d), lambda i: (0, 0)),
                  pl.BlockSpec((d, 2 * D_MEM), lambda i: (0, 0))],
        out_specs=[pl.BlockSpec((1, m, D_MEM), lambda i: (i, 0, 0)),
                   pl.BlockSpec((1, D_MEM, m), lambda i: (i, 0, 0))],
        out_shape=[jax.ShapeDtypeStruct((b, m, D_MEM), BF16),
                   jax.ShapeDtypeStruct((b, D_MEM, m), BF16)],
        compiler_params=_params(1),
        name="mem_kv",
    )(mem, g, w)


def _layer_kernel(xf_ref, xa_ref, g_ref, w_ref, lng_ref, lnb_ref, ws_ref, sbx_ref, avg_ref, bga_ref,
                  bgm_ref, km_ref, vmt_ref, bias_ref, cfar_ref, bgb_ref, wo_ref, fg_ref,
                  o_ref,
                  k_scr, vt_scr, kmean_scr,
                  q_buf, ya_buf, ym_buf, z_buf,
                  u_scr, v_scr, vtmp_ref, qm_scr, omt_ref,
                  qh_ref, sel_ref, m_ref, l_ref, ot_ref, sa_ref, sb_ref,
                  *, n_tiles, n_blocks, final):
    g = pl.program_id(0)
    blk = MOBA_BLOCK
    f = jnp.minimum(g, n_tiles - 1)
    a = jnp.maximum(g - 1, 0)
    tf = f % n_blocks
    pf = (f // n_blocks) % 2
    t = a % n_blocks
    pa = (a // n_blocks) % 2
    pw = g % 2
    pr = 1 - pw

    @pl.when(g == 0)
    def _():
        q_buf[1] = jnp.zeros((blk, D_MOBA), BF16)
        ya_buf[1] = jnp.zeros((blk, D_GMLP), BF16)
        ym_buf[1] = jnp.zeros((blk, D_MEM), BF16)
        z_buf[1] = jnp.zeros((blk, D_MIX), BF16)
        kmean_scr[...] = jnp.zeros_like(kmean_scr)
        k_scr[0] = jnp.zeros((blk, D_MOBA), BF16)
        vt_scr[0] = jnp.zeros((D_MOBA, blk), BF16)

    h_in = _rms(xf_ref[0], g_ref[...]).astype(BF16)
    scale = HEAD_DIM ** -0.5 * LOG2E

    def seg(i, lo=0, hi=None):
        c0 = _OFF[i] + lo
        c1 = _OFF[i + 1] if hi is None else _OFF[i] + hi
        return _dot(h_in, w_ref[:, c0:c1])

    def f_u():
        u_scr[...] = seg(0)

    def f_v():
        v_scr[...] = seg(1)

    def f_qm():
        qm_scr[...] = (seg(5) * scale).astype(BF16)

    def f_q():
        q_buf[pw] = (seg(2) * scale).astype(BF16)

    def f_k():
        kf = seg(3)
        k_scr[pf * n_blocks + tf] = kf.astype(BF16)
        mean_rows = lax.broadcasted_iota(jnp.int32, (SEL_ROWS, D_MOBA), 0)
        kmean_scr[pf] = jnp.where(mean_rows == tf, jnp.mean(kf, axis=0, keepdims=True), kmean_scr[pf])

    def f_vt():
        vtmp_ref[...] = seg(4)
        vt_scr[pf * n_blocks + tf] = vtmp_ref[...].T.astype(BF16)

    def f_z(i):
        w = D_MIX // 4
        z_buf[pw, :, i * w:(i + 1) * w] = seg(6, i * w, (i + 1) * w).astype(BF16)

    n_chunks = blk // GMLP_CHUNK
    gm = {}
    lane_c = lax.broadcasted_iota(jnp.int32, (GMLP_CHUNK, LANES), 1)
    first_head = lane_c < HEAD_DIM

    def chunk_rows(c):
        return slice(c * GMLP_CHUNK, (c + 1) * GMLP_CHUNK)

    def gm_a():
        for c in range(n_chunks):
            vc = jax.nn.gelu(v_scr[chunk_rows(c), :])
            gm[c] = (vc, _dot(vc.astype(BF16), avg_ref[...]))

    def gm_b():
        for c in range(n_chunks):
            vc, mu = gm[c]
            dv = vc - mu
            gm[c] = (dv, _dot((dv * dv).astype(BF16), avg_ref[...]))

    def gm_c():
        for c in range(n_chunks):
            dv, var = gm[c]
            vn = (dv * lax.rsqrt(var + EPS) * lng_ref[...] + lnb_ref[...]).astype(BF16)
            parts = []
            for p in range(GMLP_GROUPS // 2):
                vp = vn[:, p * LANES:(p + 1) * LANES]
                parts.append(jnp.where(first_head, _dot(ws_ref[2 * p], vp), _dot(ws_ref[2 * p + 1], vp)))
            gm[c] = parts

    def gm_d():
        for c in range(n_chunks):
            sv = jnp.concatenate(gm[c], axis=-1) + sbx_ref[...]
            uc = jax.nn.gelu(u_scr[chunk_rows(c), :])
            ya_buf[pw, chunk_rows(c), :] = _rms(uc * sv, bga_ref[...]).astype(BF16)

    ma = {}

    def ma_a():
        for hh in range(MEM_HEADS):
            q_h = _isolate_head(qm_scr[:, _head_lanes(hh)], hh)
            ma[hh] = _dot_nt(km_ref[0, :, _head_lanes(hh)], q_h)

    def ma_b():
        for hh in range(MEM_HEADS):
            s = ma[hh]
            p = jnp.exp2(s - jnp.max(s, axis=0, keepdims=True))
            o = _dot(vmt_ref[0, _head_rows(hh), :], p.astype(BF16))
            omt_ref[_head_rows(hh), :] = o * (1.0 / jnp.sum(p, axis=0, keepdims=True))

    def ma_c():
        ym_buf[pw] = _rms(omt_ref[...].T, bgm_ref[...]).astype(BF16)

    row_id = lax.broadcasted_iota(jnp.int32, (SEL_ROWS, blk), 0)
    kpos = lax.broadcasted_iota(jnp.int32, (blk, blk), 0)
    qpos = lax.broadcasted_iota(jnp.int32, (blk, blk), 1)
    causal_t = kpos <= qpos
    jp = jnp.maximum(t - 1, 0)

    def qk(j, h):
        return _dot_nt(k_scr[pa * n_blocks + j, :, _head_lanes(h)], qh_ref[h])

    def update(h, s, shift, j):
        m_old = m_ref[h]
        m_new = jnp.maximum(m_old, jnp.max(s, axis=0, keepdims=True) + shift)
        alpha = jnp.exp2(m_old - m_new)
        p = jnp.exp2(s - (m_new - shift))
        m_ref[h] = m_new
        l_ref[h] = alpha * l_ref[h] + jnp.sum(p, axis=0, keepdims=True)
        ot_ref[_head_rows(h), :] = alpha * ot_ref[_head_rows(h), :] + _dot(
            vt_scr[pa * n_blocks + j, _head_rows(h), :], p.astype(BF16))

    f_u()
    f_v()
    km_all = kmean_scr[pa]
    km_hi = km_all.astype(BF16)
    km_lo = (km_all - km_hi.astype(F32)).astype(BF16)
    stage0_tasks = {2: gm_a}
    for h in range(MOBA_HEADS):
        q_h = _isolate_head(q_buf[pr, :, _head_lanes(h)], h)
        qh_ref[h] = q_h
        gs = _dot_nt(km_hi[:, _head_lanes(h)], q_h) + _dot_nt(km_lo[:, _head_lanes(h)], q_h)
        gs = jnp.where(row_id < t, gs, NEG)
        rank = jnp.zeros((SEL_ROWS, blk), jnp.int32)
        for i in range(n_blocks):
            gi = gs[i:i + 1, :]
            ahead = (gi > gs) | ((gi == gs) & (row_id > i))
            rank = rank + ahead.astype(jnp.int32)
        sel_ref[h] = ((rank < MOBA_TOPK) & (row_id < t)).astype(F32)
        sb_ref[h] = qk(t, h)
        if h in stage0_tasks:
            stage0_tasks[h]()

    stage1_tasks = {0: [f_qm], 1: [f_q, gm_b], 2: [f_k], 3: [f_vt, gm_c], 4: [], 5: [gm_d]}
    for h in range(MOBA_HEADS):
        sa_ref[h] = qk(jp, h)
        for task in stage1_tasks[h]:
            task()
        s = jnp.where(causal_t, sb_ref[h] + bias_ref[h, 0], NEG)
        m0 = jnp.max(s, axis=0, keepdims=True)
        p = jnp.exp2(s - m0)
        m_ref[h] = m0
        l_ref[h] = jnp.sum(p, axis=0, keepdims=True)
        ot_ref[_head_rows(h), :] = _dot(vt_scr[pa * n_blocks + t, _head_rows(h), :], p.astype(BF16))

    stage2_tasks = {0: [lambda: f_z(0)], 1: [lambda: f_z(1)], 2: [lambda: f_z(2), ma_a],
                    3: [lambda: f_z(3)], 4: [ma_b], 5: [ma_c]}
    for h in range(MOBA_HEADS):
        sb_ref[h] = qk(0, h)
        for task in stage2_tasks[h]:
            task()
        s = jnp.where(sel_ref[h, pl.ds(jp, 1), :] > 0.5, sa_ref[h] + bias_ref[h, 1], NEG)
        update(h, s, 0.0, jp)

    def far_step(j, cur_ref, nxt_ref):
        for h in range(MOBA_HEADS):
            nxt_ref[h] = qk(j + 1, h)
            s = jnp.where(sel_ref[h, pl.ds(j, 1), :] > 0.5, cur_ref[h], NEG)
            update(h, s, cfar_ref[h], j)

    def far(j, carry):
        @pl.when(j % 2 == 0)
        def _():
            far_step(j, sb_ref, sa_ref)

        @pl.when(j % 2 == 1)
        def _():
            far_step(j, sa_ref, sb_ref)

        return carry

    lax.fori_loop(0, t - 1, far, 0)

    for h in range(MOBA_HEADS):
        ot_ref[_head_rows(h), :] = ot_ref[_head_rows(h), :] * (1.0 / l_ref[h])
    yb = _rms(ot_ref[...].T, bgb_ref[...]).astype(BF16)
    y = jnp.concatenate([ya_buf[pr], yb, ym_buf[pr]], axis=-1).astype(F32)
    y = (y * jax.nn.silu(z_buf[pr].astype(F32))).astype(BF16)
    out = xa_ref[0] + _dot(y, wo_ref[...])
    if final:
        out = _rms(out, fg_ref[...])
    o_ref[0] = out


def _layer(x, km, vmt, g_in, w_in, ln_g, ln_b, w_causal, sb_exp, avg, bga, bgm, bias, cfar, bgb,
           w_out, fg, final):
    b, s, d = x.shape
    m = km.shape[1]
    blk = MOBA_BLOCK
    nb = s // blk
    n_tiles = b * nb
    full = lambda arr: pl.BlockSpec(arr.shape, lambda g: (0,) * arr.ndim)

    def front_tile(g):
        f = jnp.minimum(g, n_tiles - 1)
        return f // nb, f % nb

    def back_tile(g):
        a = jnp.maximum(g - 1, 0)
        return a // nb, a % nb

    x_front = pl.BlockSpec((1, blk, d), lambda g: (*front_tile(g), 0))
    x_back = pl.BlockSpec((1, blk, d), lambda g: (*back_tile(g), 0))
    mem_spec = lambda shape: pl.BlockSpec(shape, lambda g: (front_tile(g)[0], 0, 0))
    vm = lambda shape, dt: pltpu.VMEM(shape, dt)
    return pl.pallas_call(
        functools.partial(_layer_kernel, n_tiles=n_tiles, n_blocks=nb, final=final),
        grid=(n_tiles + 1,),
        in_specs=[x_front, x_back, full(g_in), full(w_in), full(ln_g), full(ln_b), full(w_causal),
                  full(sb_exp), full(avg), full(bga), full(bgm),
                  mem_spec((1, m, D_MEM)), mem_spec((1, D_MEM, m)),
                  full(bias), full(cfar), full(bgb), full(w_out), full(fg)],
        out_specs=x_back,
        out_shape=jax.ShapeDtypeStruct((b, s, d), F32),
        scratch_shapes=[
            vm((2 * nb, blk, D_MOBA), BF16),
            vm((2 * nb, D_MOBA, blk), BF16),
            vm((2, SEL_ROWS, D_MOBA), F32),
            vm((2, blk, D_MOBA), BF16), vm((2, blk, D_GMLP), BF16), vm((2, blk, D_MEM), BF16),
            vm((2, blk, D_MIX), BF16),
            vm((blk, D_GMLP), F32), vm((blk, D_GMLP), F32),
            vm((blk, D_MOBA), F32),
            vm((blk, D_MEM), BF16),
            vm((D_MEM, blk), F32),
            vm((MOBA_HEADS, blk, LANES), BF16),
            vm((MOBA_HEADS, SEL_ROWS, blk), F32),
            vm((MOBA_HEADS, 1, blk), F32),
            vm((MOBA_HEADS, 1, blk), F32),
            vm((D_MOBA, blk), F32),
            vm((MOBA_HEADS, blk, blk), F32),
            vm((MOBA_HEADS, blk, blk), F32)],
        compiler_params=_params(1),
        name="layer",
    )(x, x, g_in, w_in, ln_g, ln_b, w_causal, sb_exp, avg, bga, bgm, km, vmt, bias, cfar, bgb, w_out, fg)


def _t5_bucket(dist):
    dist = jnp.maximum(dist, 0)
    max_exact = REL_BUCKETS // 2
    df = jnp.maximum(dist, 1).astype(F32)
    large = max_exact + (jnp.log(df / max_exact) / math.log(REL_MAX_DIST / max_exact)
                         * (REL_BUCKETS - max_exact)).astype(jnp.int32)
    large = jnp.minimum(large, REL_BUCKETS - 1)
    return jnp.where(dist < max_exact, dist, large)


def _bias_tables(rel_bias):
    assert MOBA_BLOCK + 1 >= REL_MAX_DIST
    kq = jnp.arange(MOBA_BLOCK)
    d_own = kq[None, :] - kq[:, None]
    buckets = jnp.stack([_t5_bucket(d_own), _t5_bucket(d_own + MOBA_BLOCK)])
    onehot = (buckets[None] == jnp.arange(REL_BUCKETS)[:, None, None, None]).astype(F32)
    table = rel_bias.astype(F32) * LOG2E
    tiles = jnp.einsum('nh,nxkq->hxkq', table, onehot, precision=lax.Precision.HIGHEST)
    cfar = jnp.broadcast_to(table[REL_BUCKETS - 1][:, None, None], (MOBA_HEADS, 1, MOBA_BLOCK))
    return tiles, cfar


def kernel(x, mem, norm_g, w_in, gmlp_ln_g, gmlp_ln_b, spatial_w, spatial_b, rel_bias,
           mem_norm_g, w_mem_kv, branch_norm_g, w_out, final_norm_g):
    b, s, d = x.shape
    depth = w_in.shape[0]

    bias, cfar = _bias_tables(rel_bias)
    causal = jnp.tril(jnp.ones((GMLP_CHUNK, GMLP_CHUNK), F32))
    grp = np.arange(D_GMLP) // HEAD_DIM
    avg = jnp.asarray((grp[:, None] == grp[None, :]).astype(np.float32) / HEAD_DIM, BF16)
    fg = final_norm_g.reshape(1, d)

    for l in range(depth):
        km, vmt = _mem_kv(mem, mem_norm_g[l].reshape(1, d), w_mem_kv[l].astype(BF16))
        bg = branch_norm_g[l]
        sb_exp = jnp.repeat(spatial_b[l].T, HEAD_DIM, axis=1)
        x = _layer(x, km, vmt, norm_g[l].reshape(1, d), w_in[l].astype(BF16),
                   gmlp_ln_g[l].reshape(1, D_GMLP), gmlp_ln_b[l].reshape(1, D_GMLP),
                   (spatial_w[l] * causal[None]).astype(BF16), sb_exp, avg,
                   bg[:D_GMLP].reshape(1, D_GMLP), bg[D_GMLP + D_MOBA:].reshape(1, D_MEM),
                   bias, cfar, bg[D_GMLP:D_GMLP + D_MOBA].reshape(1, D_MOBA),
                   w_out[l].astype(BF16), fg, final=(l == depth - 1))
    return x
```

```python
import functools
import math

import numpy as np
import jax
import jax.numpy as jnp
from jax import lax
from jax.experimental import pallas as pl
from jax.experimental.pallas import tpu as pltpu

D_MODEL = 1024
HEAD_DIM = 64
GMLP_GROUPS = 6
MOBA_HEADS = 6
MEM_HEADS = 4
D_GMLP = GMLP_GROUPS * HEAD_DIM
D_MOBA = MOBA_HEADS * HEAD_DIM
D_MEM = MEM_HEADS * HEAD_DIM
D_MIX = D_GMLP + D_MOBA + D_MEM
D_IN = 2 * D_GMLP + 3 * D_MOBA + D_MEM + D_MIX
GMLP_CHUNK = 128
MOBA_BLOCK = 256
MOBA_TOPK = 3
REL_BUCKETS = 32
REL_MAX_DIST = 128
EPS = 1e-6
NEG = -1e30
LOG2E = math.log2(math.e)

LANES = 128
SEL_ROWS = 16
VMEM_LIMIT = 56 * 1024 * 1024

F32 = jnp.float32
BF16 = jnp.bfloat16

_OFF = np.cumsum([0, D_GMLP, D_GMLP, D_MOBA, D_MOBA, D_MOBA, D_MEM, D_MIX])


def _params(n_axes):
    return pltpu.CompilerParams(
        dimension_semantics=("arbitrary",) * n_axes, vmem_limit_bytes=VMEM_LIMIT)


def _rms(x, g):
    return x * lax.rsqrt(jnp.mean(x * x, axis=-1, keepdims=True) + EPS) * g


def _dot(a, b):
    return jnp.dot(a, b, preferred_element_type=F32)


def _dot_nt(a, b):
    return lax.dot_general(a, b, (((1,), (1,)), ((), ())), preferred_element_type=F32)


def _head_lanes(h):
    return slice((h // 2) * LANES, (h // 2 + 1) * LANES)


def _head_rows(h):
    return slice(h * HEAD_DIM, (h + 1) * HEAD_DIM)


ONES_ROWS = 16
ACC_ROWS = HEAD_DIM + ONES_ROWS


def _acc_rows(h):
    return slice(h * ACC_ROWS, (h + 1) * ACC_ROWS)


def _isolate_head(pair, h):
    lane = lax.broadcasted_iota(jnp.int32, pair.shape, 1)
    keep = (lane < HEAD_DIM) if h % 2 == 0 else (lane >= HEAD_DIM)
    return jnp.where(keep, pair, jnp.zeros((), pair.dtype))


def _mem_kv_kernel(mem_ref, g_ref, w_ref, km_ref, vmt_ref):
    mn = _rms(mem_ref[0], g_ref[...]).astype(BF16)
    kv = _dot(mn, w_ref[...])
    km_ref[0] = kv[:, :D_MEM].astype(BF16)
    vmt_ref[0] = kv[:, D_MEM:].T.astype(BF16)


def _mem_kv(mem, g, w):
    b, m, d = mem.shape
    return pl.pallas_call(
        _mem_kv_kernel,
        grid=(b,),
        in_specs=[pl.BlockSpec((1, m, d), lambda i: (i, 0, 0)),
                  pl.BlockSpec((1, d), lambda i: (0, 0)),
                  pl.BlockSpec((d, 2 * D_MEM), lambda i: (0, 0))],
        out_specs=[pl.BlockSpec((1, m, D_MEM), lambda i: (i, 0, 0)),
                   pl.BlockSpec((1, D_MEM, m), lambda i: (i, 0, 0))],
        out_shape=[jax.ShapeDtypeStruct((b, m, D_MEM), BF16),
                   jax.ShapeDtypeStruct((b, D_MEM, m), BF16)],
        compiler_params=_params(1),
        name="mem_kv",
    )(mem, g, w)


def _layer_kernel(xf_ref, xa_ref, g_ref, w_ref, lng_ref, lnb_ref, ws_ref, sbx_ref, avg_ref, bga_ref,
                  bgm_ref, km_ref, vmt_ref, bias_ref, cfar_ref, bgb_ref, wo_ref, fg_ref,
                  o_ref,
                  k_scr, vt_scr, kmean_scr,
                  q_buf, ya_buf, ym_buf, z_buf,
                  u_scr, v_scr, vtmp_ref, qm_scr, omt_ref,
                  qh_ref, sel_ref, m_ref, ot_ref, sa_ref, sb_ref,
                  *, n_tiles, n_blocks, final):
    g = pl.program_id(0)
    blk = MOBA_BLOCK
    f = jnp.minimum(g, n_tiles - 1)
    a = jnp.maximum(g - 1, 0)
    tf = f % n_blocks
    pf = (f // n_blocks) % 2
    t = a % n_blocks
    pa = (a // n_blocks) % 2
    pw = g % 2
    pr = 1 - pw

    @pl.when(g == 0)
    def _():
        q_buf[1] = jnp.zeros((blk, D_MOBA), BF16)
        ya_buf[1] = jnp.zeros((blk, D_GMLP), BF16)
        ym_buf[1] = jnp.zeros((blk, D_MEM), BF16)
        z_buf[1] = jnp.zeros((blk, D_MIX), BF16)
        kmean_scr[...] = jnp.zeros_like(kmean_scr)
        k_scr[0] = jnp.zeros((blk, D_MOBA), BF16)
        vt_scr[...] = jnp.ones(vt_scr.shape, BF16)

    h_in = _rms(xf_ref[0], g_ref[...]).astype(BF16)
    scale = HEAD_DIM ** -0.5 * LOG2E

    def seg(i, lo=0, hi=None):
        c0 = _OFF[i] + lo
        c1 = _OFF[i + 1] if hi is None else _OFF[i] + hi
        return _dot(h_in, w_ref[:, c0:c1])

    def f_u():
        u_scr[...] = seg(0)

    def f_v():
        v_scr[...] = seg(1)

    def f_qm():
        qm_scr[...] = (seg(5) * scale).astype(BF16)

    def f_q():
        q_buf[pw] = (seg(2) * scale).astype(BF16)

    def f_k():
        kf = seg(3)
        k_scr[pf * n_blocks + tf] = kf.astype(BF16)
        mean_rows = lax.broadcasted_iota(jnp.int32, (SEL_ROWS, D_MOBA), 0)
        kmean_scr[pf] = jnp.where(mean_rows == tf, jnp.mean(kf, axis=0, keepdims=True), kmean_scr[pf])

    def f_vt():
        vtmp_ref[...] = seg(4)
        vt = vtmp_ref[...].T.astype(BF16)
        for hh in range(MOBA_HEADS):
            vt_scr[pf * n_blocks + tf, hh * ACC_ROWS:hh * ACC_ROWS + HEAD_DIM, :] = vt[_head_rows(hh), :]

    def f_z(i):
        w = D_MIX // 4
        z_buf[pw, :, i * w:(i + 1) * w] = seg(6, i * w, (i + 1) * w).astype(BF16)

    n_chunks = blk // GMLP_CHUNK
    gm = {}
    lane_c = lax.broadcasted_iota(jnp.int32, (GMLP_CHUNK, LANES), 1)
    first_head = lane_c < HEAD_DIM

    def chunk_rows(c):
        return slice(c * GMLP_CHUNK, (c + 1) * GMLP_CHUNK)

    def gm_a():
        for c in range(n_chunks):
            vc = jax.nn.gelu(v_scr[chunk_rows(c), :])
            gm[c] = (vc, _dot(vc.astype(BF16), avg_ref[...]))

    def gm_b():
        for c in range(n_chunks):
            vc, mu = gm[c]
            dv = vc - mu
            gm[c] = (dv, _dot((dv * dv).astype(BF16), avg_ref[...]))

    def gm_c():
        vns = []
        for c in range(n_chunks):
            dv, var = gm[c]
            vns.append((dv * lax.rsqrt(var + EPS) * lng_ref[...] + lnb_ref[...]).astype(BF16))
        zero = jnp.zeros((), BF16)
        outs = []
        for p in range(GMLP_GROUPS // 2):
            vps = [vn[:, p * LANES:(p + 1) * LANES] for vn in vns]
            first = jnp.concatenate([jnp.where(first_head, vp, zero) for vp in vps], axis=1)
            second = jnp.concatenate([jnp.where(first_head, zero, vp) for vp in vps], axis=1)
            outs.append(_dot(ws_ref[p], jnp.concatenate([first, second], axis=0)))
        gm["sv"] = outs

    def gm_d():
        for c in range(n_chunks):
            sv = jnp.concatenate([o[:, c * LANES:(c + 1) * LANES] for o in gm["sv"]], axis=-1) + sbx_ref[...]
            uc = jax.nn.gelu(u_scr[chunk_rows(c), :])
            ya_buf[pw, chunk_rows(c), :] = _rms(uc * sv, bga_ref[...]).astype(BF16)

    ma = {}

    def ma_a():
        for hh in range(MEM_HEADS):
            q_h = _isolate_head(qm_scr[:, _head_lanes(hh)], hh)
            ma[hh] = _dot_nt(km_ref[0, :, _head_lanes(hh)], q_h)

    def ma_b():
        for hh in range(MEM_HEADS):
            s = ma[hh]
            p = jnp.exp2(s - jnp.max(s, axis=0, keepdims=True))
            o = _dot(vmt_ref[0, _head_rows(hh), :], p.astype(BF16))
            omt_ref[_head_rows(hh), :] = o * (1.0 / jnp.sum(p, axis=0, keepdims=True))

    def ma_c():
        ym_buf[pw] = _rms(omt_ref[...].T, bgm_ref[...]).astype(BF16)

    row_id = lax.broadcasted_iota(jnp.int32, (SEL_ROWS, blk), 0)
    kpos = lax.broadcasted_iota(jnp.int32, (blk, blk), 0)
    qpos = lax.broadcasted_iota(jnp.int32, (blk, blk), 1)
    causal_t = kpos <= qpos
    jp = jnp.maximum(t - 1, 0)

    def qk(j, h):
        return _dot_nt(k_scr[pa * n_blocks + j, :, _head_lanes(h)], qh_ref[h])

    def update(h, s, shift, j):
        sel = sel_ref[h, pl.ds(j, 1), :] > 0.5
        m_old = m_ref[h]
        m_new = jnp.where(sel, jnp.maximum(m_old, jnp.max(s, axis=0, keepdims=True) + shift), m_old)
        alpha = jnp.exp2(m_old - m_new)
        p = jnp.exp2(s - jnp.where(sel, m_new - shift, -NEG))
        m_ref[h] = m_new
        ot_ref[_acc_rows(h), :] = alpha * ot_ref[_acc_rows(h), :] + _dot(
            vt_scr[pa * n_blocks + j, _acc_rows(h), :], p.astype(BF16))

    for h in range(MOBA_HEADS):
        qh_ref[h] = _isolate_head(q_buf[pr, :, _head_lanes(h)], h)
    km_all = kmean_scr[pa]
    km_hi = km_all.astype(BF16)
    km_lo = (km_all - km_hi.astype(F32)).astype(BF16)
    gs_all = []
    for h in range(MOBA_HEADS):
        keys = jnp.concatenate([k_scr[pa * n_blocks + t, :, _head_lanes(h)],
                                km_hi[:, _head_lanes(h)], km_lo[:, _head_lanes(h)]], axis=0)
        s_ext = _dot_nt(keys, qh_ref[h])
        sb_ref[h] = s_ext[:blk]
        gs_all.append(s_ext[blk:blk + SEL_ROWS] + s_ext[blk + SEL_ROWS:])
    f_u()
    f_v()
    for h in range(MOBA_HEADS):
        gs = jnp.where(row_id < t, gs_all[h], NEG)
        rank = jnp.zeros((SEL_ROWS, blk), jnp.int32)
        for i in range(n_blocks):
            gi = gs[i:i + 1, :]
            ahead = (gi > gs) | ((gi == gs) & (row_id > i))
            rank = rank + ahead.astype(jnp.int32)
        sel_ref[h] = ((rank < MOBA_TOPK) & (row_id < t)).astype(F32)
    gm_a()

    stage1_tasks = {0: [f_q, f_k], 1: [gm_b], 2: [f_vt, f_qm], 3: [gm_c], 4: [], 5: [gm_d]}
    for h in range(MOBA_HEADS):
        sa_ref[h] = qk(0, h)
        for task in stage1_tasks[h]:
            task()
        s = jnp.where(causal_t, sb_ref[h] + bias_ref[h, 0], NEG)
        m0 = jnp.max(s, axis=0, keepdims=True)
        p = jnp.exp2(s - m0)
        m_ref[h] = m0
        ot_ref[_acc_rows(h), :] = _dot(vt_scr[pa * n_blocks + t, _acc_rows(h), :], p.astype(BF16))

    def far_step(j, cur_ref, nxt_ref):
        order = [("qk", 0), ("qk", 1), ("qk", 2), ("qk", 3), ("pv", 0), ("pv", 1), ("qk", 4), ("qk", 5),
                 ("pv", 2), ("pv", 3), ("pv", 4), ("pv", 5)]
        for kind, h in order:
            if kind == "qk":
                nxt_ref[h] = qk(j + 1, h)
            else:
                update(h, cur_ref[h], cfar_ref[h], j)

    def far(j, carry):
        @pl.when(j % 2 == 0)
        def _():
            far_step(j, sa_ref, sb_ref)

        @pl.when(j % 2 == 1)
        def _():
            far_step(j, sb_ref, sa_ref)

        return carry

    lax.fori_loop(0, t - 1, far, 0)

    def prev_step(cur_ref):
        tasks = {0: [lambda: f_z(0), lambda: f_z(1)], 1: [ma_a], 2: [lambda: f_z(2), lambda: f_z(3)],
                 3: [ma_b], 4: [], 5: [ma_c]}
        for h in range(MOBA_HEADS):
            for task in tasks[h]:
                task()
            update(h, cur_ref[h] + bias_ref[h, 1], 0.0, jp)

    @pl.when(jp % 2 == 0)
    def _():
        prev_step(sa_ref)

    @pl.when(jp % 2 == 1)
    def _():
        prev_step(sb_ref)

    heads_t = []
    for h in range(MOBA_HEADS):
        acc = ot_ref[_acc_rows(h), :]
        heads_t.append(acc[:HEAD_DIM] * (1.0 / acc[HEAD_DIM:HEAD_DIM + 1]))
    yb = _rms(jnp.concatenate(heads_t, axis=0).T, bgb_ref[...]).astype(BF16)
    y = jnp.concatenate([ya_buf[pr], yb, ym_buf[pr]], axis=-1).astype(F32)
    y = (y * jax.nn.silu(z_buf[pr].astype(F32))).astype(BF16)
    out = xa_ref[0] + _dot(y, wo_ref[...])
    if final:
        out = _rms(out, fg_ref[...])
    o_ref[0] = out


def _layer(x, km, vmt, g_in, w_in, ln_g, ln_b, w_causal, sb_exp, avg, bga, bgm, bias, cfar, bgb,
           w_out, fg, final):
    b, s, d = x.shape
    m = km.shape[1]
    blk = MOBA_BLOCK
    nb = s // blk
    n_tiles = b * nb
    full = lambda arr: pl.BlockSpec(arr.shape, lambda g: (0,) * arr.ndim)

    def front_tile(g):
        f = jnp.minimum(g, n_tiles - 1)
        return f // nb, f % nb

    def back_tile(g):
        a = jnp.maximum(g - 1, 0)
        return a // nb, a % nb

    x_front = pl.BlockSpec((1, blk, d), lambda g: (*front_tile(g), 0))
    x_back = pl.BlockSpec((1, blk, d), lambda g: (*back_tile(g), 0))
    mem_spec = lambda shape: pl.BlockSpec(shape, lambda g: (front_tile(g)[0], 0, 0))
    vm = lambda shape, dt: pltpu.VMEM(shape, dt)
    return pl.pallas_call(
        functools.partial(_layer_kernel, n_tiles=n_tiles, n_blocks=nb, final=final),
        grid=(n_tiles + 1,),
        in_specs=[x_front, x_back, full(g_in), full(w_in), full(ln_g), full(ln_b), full(w_causal),
                  full(sb_exp), full(avg), full(bga), full(bgm),
                  mem_spec((1, m, D_MEM)), mem_spec((1, D_MEM, m)),
                  full(bias), full(cfar), full(bgb), full(w_out), full(fg)],
        out_specs=x_back,
        out_shape=jax.ShapeDtypeStruct((b, s, d), F32),
        scratch_shapes=[
            vm((2 * nb, blk, D_MOBA), BF16),
            vm((2 * nb, MOBA_HEADS * ACC_ROWS, blk), BF16),
            vm((2, SEL_ROWS, D_MOBA), F32),
            vm((2, blk, D_MOBA), BF16), vm((2, blk, D_GMLP), BF16), vm((2, blk, D_MEM), BF16),
            vm((2, blk, D_MIX), BF16),
            vm((blk, D_GMLP), F32), vm((blk, D_GMLP), F32),
            vm((blk, D_MOBA), F32),
            vm((blk, D_MEM), BF16),
            vm((D_MEM, blk), F32),
            vm((MOBA_HEADS, blk, LANES), BF16),
            vm((MOBA_HEADS, SEL_ROWS, blk), F32),
            vm((MOBA_HEADS, 1, blk), F32),
            vm((MOBA_HEADS * ACC_ROWS, blk), F32),
            vm((MOBA_HEADS, blk, blk), F32),
            vm((MOBA_HEADS, blk, blk), F32)],
        compiler_params=_params(1),
        name="layer",
    )(x, x, g_in, w_in, ln_g, ln_b, w_causal, sb_exp, avg, bga, bgm, km, vmt, bias, cfar, bgb, w_out, fg)


def _t5_bucket(dist):
    dist = jnp.maximum(dist, 0)
    max_exact = REL_BUCKETS // 2
    df = jnp.maximum(dist, 1).astype(F32)
    large = max_exact + (jnp.log(df / max_exact) / math.log(REL_MAX_DIST / max_exact)
                         * (REL_BUCKETS - max_exact)).astype(jnp.int32)
    large = jnp.minimum(large, REL_BUCKETS - 1)
    return jnp.where(dist < max_exact, dist, large)


def _bias_tables(rel_bias):
    assert MOBA_BLOCK + 1 >= REL_MAX_DIST
    kq = jnp.arange(MOBA_BLOCK)
    d_own = kq[None, :] - kq[:, None]
    buckets = jnp.stack([_t5_bucket(d_own), _t5_bucket(d_own + MOBA_BLOCK)])
    onehot = (buckets[None] == jnp.arange(REL_BUCKETS)[:, None, None, None]).astype(F32)
    table = rel_bias.astype(F32) * LOG2E
    tiles = jnp.einsum('nh,nxkq->hxkq', table, onehot, precision=lax.Precision.HIGHEST)
    cfar = jnp.broadcast_to(table[REL_BUCKETS - 1][:, None, None], (MOBA_HEADS, 1, MOBA_BLOCK))
    return tiles, cfar


def _pair_spatial(w):
    g, c, _ = w.shape
    return w.reshape(g // 2, 2, c, c).transpose(0, 2, 1, 3).reshape(g // 2, c, 2 * c).astype(BF16)


def kernel(x, mem, norm_g, w_in, gmlp_ln_g, gmlp_ln_b, spatial_w, spatial_b, rel_bias,
           mem_norm_g, w_mem_kv, branch_norm_g, w_out, final_norm_g):
    b, s, d = x.shape
    depth = w_in.shape[0]

    bias, cfar = _bias_tables(rel_bias)
    causal = jnp.tril(jnp.ones((GMLP_CHUNK, GMLP_CHUNK), F32))
    grp = np.arange(D_GMLP) // HEAD_DIM
    avg = jnp.asarray((grp[:, None] == grp[None, :]).astype(np.float32) / HEAD_DIM, BF16)
    fg = final_norm_g.reshape(1, d)

    for l in range(depth):
        km, vmt = _mem_kv(mem, mem_norm_g[l].reshape(1, d), w_mem_kv[l].astype(BF16))
        bg = branch_norm_g[l]
        sb_exp = jnp.repeat(spatial_b[l].T, HEAD_DIM, axis=1)
        x = _layer(x, km, vmt, norm_g[l].reshape(1, d), w_in[l].astype(BF16),
                   gmlp_ln_g[l].reshape(1, D_GMLP), gmlp_ln_b[l].reshape(1, D_GMLP),
                   _pair_spatial(spatial_w[l] * causal[None]), sb_exp, avg,
                   bg[:D_GMLP].reshape(1, D_GMLP), bg[D_GMLP + D_MOBA:].reshape(1, D_MEM),
                   bias, cfar, bg[D_GMLP:D_GMLP + D_MOBA].reshape(1, D_MOBA),
                   w_out[l].astype(BF16), fg, final=(l == depth - 1))
    return x
```

```python
import functools
import math

import numpy as np
import jax
import jax.numpy as jnp
from jax import lax
from jax.experimental import pallas as pl
from jax.experimental.pallas import tpu as pltpu

D_MODEL = 1024
HEAD_DIM = 64
GMLP_GROUPS = 6
MOBA_HEADS = 6
MEM_HEADS = 4
D_GMLP = GMLP_GROUPS * HEAD_DIM
D_MOBA = MOBA_HEADS * HEAD_DIM
D_MEM = MEM_HEADS * HEAD_DIM
D_MIX = D_GMLP + D_MOBA + D_MEM
D_IN = 2 * D_GMLP + 3 * D_MOBA + D_MEM + D_MIX
GMLP_CHUNK = 128
MOBA_BLOCK = 256
MOBA_TOPK = 3
REL_BUCKETS = 32
REL_MAX_DIST = 128
EPS = 1e-6
NEG = -1e30
LOG2E = math.log2(math.e)

LANES = 128
SEL_ROWS = 16
VMEM_LIMIT = 56 * 1024 * 1024

F32 = jnp.float32
BF16 = jnp.bfloat16

_OFF = np.cumsum([0, D_GMLP, D_GMLP, D_MOBA, D_MOBA, D_MOBA, D_MEM, D_MIX])


def _params(n_axes):
    return pltpu.CompilerParams(
        dimension_semantics=("arbitrary",) * n_axes, vmem_limit_bytes=VMEM_LIMIT)


def _rms(x, g):
    return x * lax.rsqrt(jnp.mean(x * x, axis=-1, keepdims=True) + EPS) * g


def _dot(a, b):
    return jnp.dot(a, b, preferred_element_type=F32)


def _dot_nt(a, b):
    return lax.dot_general(a, b, (((1,), (1,)), ((), ())), preferred_element_type=F32)


def _head_lanes(h):
    return slice((h // 2) * LANES, (h // 2 + 1) * LANES)


def _head_rows(h):
    return slice(h * HEAD_DIM, (h + 1) * HEAD_DIM)


ONES_ROWS = 16
ACC_ROWS = HEAD_DIM + ONES_ROWS


def _acc_rows(h):
    return slice(h * ACC_ROWS, (h + 1) * ACC_ROWS)


def _isolate_head(pair, h):
    lane = lax.broadcasted_iota(jnp.int32, pair.shape, 1)
    keep = (lane < HEAD_DIM) if h % 2 == 0 else (lane >= HEAD_DIM)
    return jnp.where(keep, pair, jnp.zeros((), pair.dtype))


def _mem_kv_kernel(mem_ref, g_ref, w_ref, km_ref, vmt_ref):
    mn = _rms(mem_ref[0], g_ref[...]).astype(BF16)
    kv = _dot(mn, w_ref[...])
    km_ref[0] = kv[:, :D_MEM].astype(BF16)
    vmt_ref[0] = kv[:, D_MEM:].T.astype(BF16)


def _mem_kv(mem, g, w):
    b, m, d = mem.shape
    return pl.pallas_call(
        _mem_kv_kernel,
        grid=(b,),
        in_specs=[pl.BlockSpec((1, m, d), lambda i: (i, 0, 0)),
                  pl.BlockSpec((1, d), lambda i: (0, 0)),
                  pl.BlockSpec((d, 2 * D_MEM), lambda i: (0, 0))],
        out_specs=[pl.BlockSpec((1, m, D_MEM), lambda i: (i, 0, 0)),
                   pl.BlockSpec((1, D_MEM, m), lambda i: (i, 0, 0))],
        out_shape=[jax.ShapeDtypeStruct((b, m, D_MEM), BF16),
                   jax.ShapeDtypeStruct((b, D_MEM, m), BF16)],
        compiler_params=_params(1),
        name="mem_kv",
    )(mem, g, w)


def _layer_kernel(xn_ref, xa_ref, g_ref, w_ref, lng_ref, lnb_ref, ws_ref, sbx_ref, avg_ref, bga_ref,
                  bgm_ref, km_ref, vmt_ref, bias_ref, cfar_ref, bgb_ref, wo_ref, fg_ref,
                  o_ref,
                  k_scr, vt_scr, kmean_scr,
                  q_buf, ya_buf, ym_buf, z_buf,
                  u_scr, v_scr, vtmp_ref, qm_scr, omt_ref,
                  qh_ref, sel_ref, m_ref, ot_ref, sa_ref, sb_ref, h_buf,
                  *, n_tiles, n_blocks, final):
    g = pl.program_id(0)
    blk = MOBA_BLOCK
    f = jnp.minimum(g, n_tiles - 1)
    a = jnp.maximum(g - 1, 0)
    tf = f % n_blocks
    pf = (f // n_blocks) % 2
    t = a % n_blocks
    pa = (a // n_blocks) % 2
    pw = g % 2
    pr = 1 - pw

    @pl.when(g == 0)
    def _():
        q_buf[1] = jnp.zeros((blk, D_MOBA), BF16)
        ya_buf[1] = jnp.zeros((blk, D_GMLP), BF16)
        ym_buf[1] = jnp.zeros((blk, D_MEM), BF16)
        z_buf[1] = jnp.zeros((blk, D_MIX), BF16)
        kmean_scr[...] = jnp.zeros_like(kmean_scr)
        k_scr[0] = jnp.zeros((blk, D_MOBA), BF16)
        vt_scr[...] = jnp.ones(vt_scr.shape, BF16)
        h_buf[1] = _rms(xa_ref[0], g_ref[...]).astype(BF16)

    def f_norm_next():
        h_buf[pw] = _rms(xn_ref[0], g_ref[...]).astype(BF16)

    scale = HEAD_DIM ** -0.5 * LOG2E

    def seg(i, lo=0, hi=None):
        c0 = _OFF[i] + lo
        c1 = _OFF[i + 1] if hi is None else _OFF[i] + hi
        return _dot(h_buf[pr], w_ref[:, c0:c1])

    def f_u():
        u_scr[...] = seg(0)

    def f_v():
        v_scr[...] = seg(1)

    def f_qm():
        qm_scr[...] = (seg(5) * scale).astype(BF16)

    def f_q():
        q_buf[pw] = (seg(2) * scale).astype(BF16)

    def f_k():
        kf = seg(3)
        k_scr[pf * n_blocks + tf] = kf.astype(BF16)
        mean_rows = lax.broadcasted_iota(jnp.int32, (SEL_ROWS, D_MOBA), 0)
        kmean_scr[pf] = jnp.where(mean_rows == tf, jnp.mean(kf, axis=0, keepdims=True), kmean_scr[pf])

    def f_vt():
        vtmp_ref[...] = seg(4)
        vt = vtmp_ref[...].T.astype(BF16)
        for hh in range(MOBA_HEADS):
            vt_scr[pf * n_blocks + tf, hh * ACC_ROWS:hh * ACC_ROWS + HEAD_DIM, :] = vt[_head_rows(hh), :]

    def f_z(i):
        w = D_MIX // 4
        z_buf[pw, :, i * w:(i + 1) * w] = seg(6, i * w, (i + 1) * w).astype(BF16)

    n_chunks = blk // GMLP_CHUNK
    gm = {}
    lane_c = lax.broadcasted_iota(jnp.int32, (GMLP_CHUNK, LANES), 1)
    first_head = lane_c < HEAD_DIM

    def chunk_rows(c):
        return slice(c * GMLP_CHUNK, (c + 1) * GMLP_CHUNK)

    def gm_a():
        for c in range(n_chunks):
            vc = jax.nn.gelu(v_scr[chunk_rows(c), :])
            gm[c] = (vc, _dot(vc.astype(BF16), avg_ref[...]))

    def gm_b():
        for c in range(n_chunks):
            vc, mu = gm[c]
            dv = vc - mu
            gm[c] = (dv, _dot((dv * dv).astype(BF16), avg_ref[...]))

    def gm_c():
        vns = []
        for c in range(n_chunks):
            dv, var = gm[c]
            vns.append((dv * lax.rsqrt(var + EPS) * lng_ref[...] + lnb_ref[...]).astype(BF16))
        zero = jnp.zeros((), BF16)
        outs = []
        for p in range(GMLP_GROUPS // 2):
            vps = [vn[:, p * LANES:(p + 1) * LANES] for vn in vns]
            first = jnp.concatenate([jnp.where(first_head, vp, zero) for vp in vps], axis=1)
            second = jnp.concatenate([jnp.where(first_head, zero, vp) for vp in vps], axis=1)
            outs.append(_dot(ws_ref[p], jnp.concatenate([first, second], axis=0)))
        gm["sv"] = outs

    def gm_d():
        for c in range(n_chunks):
            sv = jnp.concatenate([o[:, c * LANES:(c + 1) * LANES] for o in gm["sv"]], axis=-1) + sbx_ref[...]
            uc = jax.nn.gelu(u_scr[chunk_rows(c), :])
            ya_buf[pw, chunk_rows(c), :] = _rms(uc * sv, bga_ref[...]).astype(BF16)

    ma = {}

    def ma_a():
        for hh in range(MEM_HEADS):
            q_h = _isolate_head(qm_scr[:, _head_lanes(hh)], hh)
            ma[hh] = _dot_nt(km_ref[0, :, _head_lanes(hh)], q_h)

    def ma_b():
        for hh in range(MEM_HEADS):
            s = ma[hh]
            p = jnp.exp2(s - jnp.max(s, axis=0, keepdims=True))
            o = _dot(vmt_ref[0, _head_rows(hh), :], p.astype(BF16))
            omt_ref[_head_rows(hh), :] = o * (1.0 / jnp.sum(p, axis=0, keepdims=True))

    def ma_c():
        ym_buf[pw] = _rms(omt_ref[...].T, bgm_ref[...]).astype(BF16)

    row_id = lax.broadcasted_iota(jnp.int32, (SEL_ROWS, blk), 0)
    kpos = lax.broadcasted_iota(jnp.int32, (blk, blk), 0)
    qpos = lax.broadcasted_iota(jnp.int32, (blk, blk), 1)
    causal_t = kpos <= qpos
    jp = jnp.maximum(t - 1, 0)

    def qk(j, h):
        return _dot_nt(k_scr[pa * n_blocks + j, :, _head_lanes(h)], qh_ref[h])

    def update(h, s, shift, j):
        sel = sel_ref[h, pl.ds(j, 1), :] > 0.5
        m_old = m_ref[h]
        m_new = jnp.where(sel, jnp.maximum(m_old, jnp.max(s, axis=0, keepdims=True) + shift), m_old)
        alpha = jnp.exp2(m_old - m_new)
        p = jnp.exp2(s - jnp.where(sel, m_new - shift, -NEG))
        m_ref[h] = m_new
        ot_ref[_acc_rows(h), :] = alpha * ot_ref[_acc_rows(h), :] + _dot(
            vt_scr[pa * n_blocks + j, _acc_rows(h), :], p.astype(BF16))

    for h in range(MOBA_HEADS):
        qh_ref[h] = _isolate_head(q_buf[pr, :, _head_lanes(h)], h)
    km_all = kmean_scr[pa]
    km_hi = km_all.astype(BF16)
    km_lo = (km_all - km_hi.astype(F32)).astype(BF16)
    gs_all = []
    for h in range(MOBA_HEADS):
        keys = jnp.concatenate([k_scr[pa * n_blocks + t, :, _head_lanes(h)],
                                km_hi[:, _head_lanes(h)], km_lo[:, _head_lanes(h)]], axis=0)
        s_ext = _dot_nt(keys, qh_ref[h])
        sb_ref[h] = s_ext[:blk]
        gs_all.append(s_ext[blk:blk + SEL_ROWS] + s_ext[blk + SEL_ROWS:])
    f_u()
    f_v()
    for h in range(MOBA_HEADS):
        gs = jnp.where(row_id < t, gs_all[h], NEG)
        rank = jnp.zeros((SEL_ROWS, blk), jnp.int32)
        for i in range(n_blocks):
            gi = gs[i:i + 1, :]
            ahead = (gi > gs) | ((gi == gs) & (row_id > i))
            rank = rank + ahead.astype(jnp.int32)
        sel_ref[h] = ((rank < MOBA_TOPK) & (row_id < t)).astype(F32)
    gm_a()

    stage1_tasks = {0: [f_q, f_k], 1: [gm_b], 2: [f_vt, f_qm], 3: [gm_c, f_norm_next], 4: [], 5: [gm_d]}
    for h in range(MOBA_HEADS):
        sa_ref[h] = qk(0, h)
        for task in stage1_tasks[h]:
            task()
        s = jnp.where(causal_t, sb_ref[h] + bias_ref[h, 0], NEG)
        m0 = jnp.max(s, axis=0, keepdims=True)
        p = jnp.exp2(s - m0)
        m_ref[h] = m0
        ot_ref[_acc_rows(h), :] = _dot(vt_scr[pa * n_blocks + t, _acc_rows(h), :], p.astype(BF16))

    def far_step(j, cur_ref, nxt_ref):
        order = [("qk", 0), ("qk", 1), ("qk", 2), ("qk", 3), ("pv", 0), ("pv", 1), ("qk", 4), ("qk", 5),
                 ("pv", 2), ("pv", 3), ("pv", 4), ("pv", 5)]
        for kind, h in order:
            if kind == "qk":
                nxt_ref[h] = qk(j + 1, h)
            else:
                update(h, cur_ref[h], cfar_ref[h], j)

    def far_pair(i, carry):
        far_step(2 * i, sa_ref, sb_ref)
        far_step(2 * i + 1, sb_ref, sa_ref)
        return carry

    n_far = jnp.maximum(t - 1, 0)
    lax.fori_loop(0, n_far // 2, far_pair, 0)

    @pl.when(n_far % 2 == 1)
    def _():
        far_step(n_far - 1, sa_ref, sb_ref)

    def prev_step(cur_ref):
        tasks = {0: [lambda: f_z(0), lambda: f_z(1)], 1: [ma_a], 2: [lambda: f_z(2), lambda: f_z(3)],
                 3: [ma_b], 4: [], 5: [ma_c]}
        for h in range(MOBA_HEADS):
            for task in tasks[h]:
                task()
            update(h, cur_ref[h] + bias_ref[h, 1], 0.0, jp)

    @pl.when(jp % 2 == 0)
    def _():
        prev_step(sa_ref)

    @pl.when(jp % 2 == 1)
    def _():
        prev_step(sb_ref)

    heads_t = []
    for h in range(MOBA_HEADS):
        acc = ot_ref[_acc_rows(h), :]
        heads_t.append(acc[:HEAD_DIM] * (1.0 / acc[HEAD_DIM:HEAD_DIM + 1]))
    yb = _rms(jnp.concatenate(heads_t, axis=0).T, bgb_ref[...]).astype(BF16)
    y = jnp.concatenate([ya_buf[pr], yb, ym_buf[pr]], axis=-1).astype(F32)
    y = (y * jax.nn.silu(z_buf[pr].astype(F32))).astype(BF16)
    out = xa_ref[0] + _dot(y, wo_ref[...])
    if final:
        out = _rms(out, fg_ref[...])
    o_ref[0] = out


def _layer(x, km, vmt, g_in, w_in, ln_g, ln_b, w_causal, sb_exp, avg, bga, bgm, bias, cfar, bgb,
           w_out, fg, final):
    b, s, d = x.shape
    m = km.shape[1]
    blk = MOBA_BLOCK
    nb = s // blk
    n_tiles = b * nb
    full = lambda arr: pl.BlockSpec(arr.shape, lambda g: (0,) * arr.ndim)

    def front_tile(g):
        f = jnp.minimum(g, n_tiles - 1)
        return f // nb, f % nb

    def back_tile(g):
        a = jnp.maximum(g - 1, 0)
        return a // nb, a % nb

    x_next = pl.BlockSpec((1, blk, d), lambda g: (*front_tile(g + 1), 0))
    x_back = pl.BlockSpec((1, blk, d), lambda g: (*back_tile(g), 0))
    mem_spec = lambda shape: pl.BlockSpec(shape, lambda g: (front_tile(g)[0], 0, 0))
    vm = lambda shape, dt: pltpu.VMEM(shape, dt)
    return pl.pallas_call(
        functools.partial(_layer_kernel, n_tiles=n_tiles, n_blocks=nb, final=final),
        grid=(n_tiles + 1,),
        in_specs=[x_next, x_back, full(g_in), full(w_in), full(ln_g), full(ln_b), full(w_causal),
                  full(sb_exp), full(avg), full(bga), full(bgm),
                  mem_spec((1, m, D_MEM)), mem_spec((1, D_MEM, m)),
                  full(bias), full(cfar), full(bgb), full(w_out), full(fg)],
        out_specs=x_back,
        out_shape=jax.ShapeDtypeStruct((b, s, d), F32),
        scratch_shapes=[
            vm((2 * nb, blk, D_MOBA), BF16),
            vm((2 * nb, MOBA_HEADS * ACC_ROWS, blk), BF16),
            vm((2, SEL_ROWS, D_MOBA), F32),
            vm((2, blk, D_MOBA), BF16), vm((2, blk, D_GMLP), BF16), vm((2, blk, D_MEM), BF16),
            vm((2, blk, D_MIX), BF16),
            vm((blk, D_GMLP), F32), vm((blk, D_GMLP), F32),
            vm((blk, D_MOBA), F32),
            vm((blk, D_MEM), BF16),
            vm((D_MEM, blk), F32),
            vm((MOBA_HEADS, blk, LANES), BF16),
            vm((MOBA_HEADS, SEL_ROWS, blk), F32),
            vm((MOBA_HEADS, 1, blk), F32),
            vm((MOBA_HEADS * ACC_ROWS, blk), F32),
            vm((MOBA_HEADS, blk, blk), F32),
            vm((MOBA_HEADS, blk, blk), F32),
            vm((2, blk, D_MODEL), BF16)],
        compiler_params=_params(1),
        name="layer",
    )(x, x, g_in, w_in, ln_g, ln_b, w_causal, sb_exp, avg, bga, bgm, km, vmt, bias, cfar, bgb, w_out, fg)


def _t5_bucket(dist):
    dist = jnp.maximum(dist, 0)
    max_exact = REL_BUCKETS // 2
    df = jnp.maximum(dist, 1).astype(F32)
    large = max_exact + (jnp.log(df / max_exact) / math.log(REL_MAX_DIST / max_exact)
                         * (REL_BUCKETS - max_exact)).astype(jnp.int32)
    large = jnp.minimum(large, REL_BUCKETS - 1)
    return jnp.where(dist < max_exact, dist, large)


def _bias_tables(rel_bias):
    assert MOBA_BLOCK + 1 >= REL_MAX_DIST
    kq = jnp.arange(MOBA_BLOCK)
    d_own = kq[None, :] - kq[:, None]
    buckets = jnp.stack([_t5_bucket(d_own), _t5_bucket(d_own + MOBA_BLOCK)])
    onehot = (buckets[None] == jnp.arange(REL_BUCKETS)[:, None, None, None]).astype(F32)
    table = rel_bias.astype(F32) * LOG2E
    tiles = jnp.einsum('nh,nxkq->hxkq', table, onehot, precision=lax.Precision.HIGHEST)
    cfar = jnp.broadcast_to(table[REL_BUCKETS - 1][:, None, None], (MOBA_HEADS, 1, MOBA_BLOCK))
    return tiles, cfar


def _pair_spatial(w):
    g, c, _ = w.shape
    return w.reshape(g // 2, 2, c, c).transpose(0, 2, 1, 3).reshape(g // 2, c, 2 * c).astype(BF16)


def kernel(x, mem, norm_g, w_in, gmlp_ln_g, gmlp_ln_b, spatial_w, spatial_b, rel_bias,
           mem_norm_g, w_mem_kv, branch_norm_g, w_out, final_norm_g):
    b, s, d = x.shape
    depth = w_in.shape[0]

    bias, cfar = _bias_tables(rel_bias)
    causal = jnp.tril(jnp.ones((GMLP_CHUNK, GMLP_CHUNK), F32))
    grp = np.arange(D_GMLP) // HEAD_DIM
    avg = jnp.asarray((grp[:, None] == grp[None, :]).astype(np.float32) / HEAD_DIM, BF16)
    fg = final_norm_g.reshape(1, d)

    for l in range(depth):
        km, vmt = _mem_kv(mem, mem_norm_g[l].reshape(1, d), w_mem_kv[l].astype(BF16))
        bg = branch_norm_g[l]
        sb_exp = jnp.repeat(spatial_b[l].T, HEAD_DIM, axis=1)
        x = _layer(x, km, vmt, norm_g[l].reshape(1, d), w_in[l].astype(BF16),
                   gmlp_ln_g[l].reshape(1, D_GMLP), gmlp_ln_b[l].reshape(1, D_GMLP),
                   _pair_spatial(spatial_w[l] * causal[None]), sb_exp, avg,
                   bg[:D_GMLP].reshape(1, D_GMLP), bg[D_GMLP + D_MOBA:].reshape(1, D_MEM),
                   bias, cfar, bg[D_GMLP:D_GMLP + D_MOBA].reshape(1, D_MOBA),
                   w_out[l].astype(BF16), fg, final=(l == depth - 1))
    return x
```

```python
import functools
import math

import numpy as np
import jax
import jax.numpy as jnp
from jax import lax
from jax.experimental import pallas as pl
from jax.experimental.pallas import tpu as pltpu

D_MODEL = 1024
HEAD_DIM = 64
GMLP_GROUPS = 6
MOBA_HEADS = 6
MEM_HEADS = 4
D_GMLP = GMLP_GROUPS * HEAD_DIM
D_MOBA = MOBA_HEADS * HEAD_DIM
D_MEM = MEM_HEADS * HEAD_DIM
D_MIX = D_GMLP + D_MOBA + D_MEM
D_IN = 2 * D_GMLP + 3 * D_MOBA + D_MEM + D_MIX
GMLP_CHUNK = 128
MOBA_BLOCK = 256
MOBA_TOPK = 3
REL_BUCKETS = 32
REL_MAX_DIST = 128
EPS = 1e-6
NEG = -1e30
LOG2E = math.log2(math.e)

LANES = 128
SEL_ROWS = 16
VMEM_LIMIT = 56 * 1024 * 1024

F32 = jnp.float32
BF16 = jnp.bfloat16

_OFF = np.cumsum([0, D_GMLP, D_GMLP, D_MOBA, D_MOBA, D_MOBA, D_MEM, D_MIX])


def _params(n_axes):
    return pltpu.CompilerParams(
        dimension_semantics=("arbitrary",) * n_axes, vmem_limit_bytes=VMEM_LIMIT)


def _rms(x, g):
    return x * lax.rsqrt(jnp.mean(x * x, axis=-1, keepdims=True) + EPS) * g


def _dot(a, b):
    return jnp.dot(a, b, preferred_element_type=F32)


def _dot_nt(a, b):
    return lax.dot_general(a, b, (((1,), (1,)), ((), ())), preferred_element_type=F32)


def _head_lanes(h):
    return slice((h // 2) * LANES, (h // 2 + 1) * LANES)


def _head_rows(h):
    return slice(h * HEAD_DIM, (h + 1) * HEAD_DIM)


ONES_ROWS = 16
ACC_ROWS = HEAD_DIM + ONES_ROWS


def _acc_rows(h):
    return slice(h * ACC_ROWS, (h + 1) * ACC_ROWS)


def _isolate_head(pair, h):
    lane = lax.broadcasted_iota(jnp.int32, pair.shape, 1)
    keep = (lane < HEAD_DIM) if h % 2 == 0 else (lane >= HEAD_DIM)
    return jnp.where(keep, pair, jnp.zeros((), pair.dtype))


def _mem_kv_kernel(mem_ref, g_ref, w_ref, km_ref, vmt_ref):
    mn = _rms(mem_ref[0], g_ref[...]).astype(BF16)
    kv = _dot(mn, w_ref[...])
    km_ref[0] = kv[:, :D_MEM].astype(BF16)
    vmt_ref[0] = kv[:, D_MEM:].T.astype(BF16)


def _mem_kv(mem, g, w):
    b, m, d = mem.shape
    return pl.pallas_call(
        _mem_kv_kernel,
        grid=(b,),
        in_specs=[pl.BlockSpec((1, m, d), lambda i: (i, 0, 0)),
                  pl.BlockSpec((1, d), lambda i: (0, 0)),
                  pl.BlockSpec((d, 2 * D_MEM), lambda i: (0, 0))],
        out_specs=[pl.BlockSpec((1, m, D_MEM), lambda i: (i, 0, 0)),
                   pl.BlockSpec((1, D_MEM, m), lambda i: (i, 0, 0))],
        out_shape=[jax.ShapeDtypeStruct((b, m, D_MEM), BF16),
                   jax.ShapeDtypeStruct((b, D_MEM, m), BF16)],
        compiler_params=_params(1),
        name="mem_kv",
    )(mem, g, w)


def _layer_kernel(xn_ref, xa_ref, g_ref, w_ref, lng_ref, lnb_ref, ws_ref, sbx_ref, avg_ref, bga_ref,
                  bgm_ref, km_ref, vmt_ref, bias_ref, cfar_ref, bgb_ref, wo_ref, fg_ref,
                  o_ref,
                  k_scr, vt_scr, kmean_scr,
                  q_buf, ya_buf, ym_buf, z_buf,
                  u_scr, v_scr, vtmp_ref, qtmp_ref, qmtmp_ref, qm_scr, omt_ref,
                  qh_ref, sel_ref, m_ref, ot_ref, sa_ref, sb_ref, h_buf,
                  *, n_tiles, n_blocks, final):
    g = pl.program_id(0)
    blk = MOBA_BLOCK
    f = jnp.minimum(g, n_tiles - 1)
    a = jnp.maximum(g - 1, 0)
    tf = f % n_blocks
    pf = (f // n_blocks) % 2
    t = a % n_blocks
    pa = (a // n_blocks) % 2
    pw = g % 2
    pr = 1 - pw

    @pl.when(g == 0)
    def _():
        q_buf[1] = jnp.zeros((D_MOBA, blk), BF16)
        ya_buf[1] = jnp.zeros((blk, D_GMLP), BF16)
        ym_buf[1] = jnp.zeros((blk, D_MEM), BF16)
        z_buf[1] = jnp.zeros((blk, D_MIX), BF16)
        kmean_scr[...] = jnp.zeros_like(kmean_scr)
        k_scr[0] = jnp.zeros((blk, D_MOBA), BF16)
        vt_scr[...] = jnp.ones(vt_scr.shape, BF16)
        h_buf[1] = _rms(xa_ref[0], g_ref[...]).astype(BF16)

    def f_norm_next():
        h_buf[pw] = _rms(xn_ref[0], g_ref[...]).astype(BF16)

    scale = HEAD_DIM ** -0.5 * LOG2E

    def seg(i, lo=0, hi=None):
        c0 = _OFF[i] + lo
        c1 = _OFF[i + 1] if hi is None else _OFF[i] + hi
        return _dot(h_buf[pr], w_ref[:, c0:c1])

    def f_u():
        u_scr[...] = seg(0)

    def f_v():
        v_scr[...] = seg(1)

    def f_qm():
        qmtmp_ref[...] = seg(5) * scale
        qm_scr[...] = qmtmp_ref[...].T.astype(BF16)

    def f_q():
        qtmp_ref[...] = seg(2) * scale
        q_buf[pw] = qtmp_ref[...].T.astype(BF16)

    def f_k():
        kf = seg(3)
        k_scr[pf * n_blocks + tf] = kf.astype(BF16)
        mean_rows = lax.broadcasted_iota(jnp.int32, (SEL_ROWS, D_MOBA), 0)
        kmean_scr[pf] = jnp.where(mean_rows == tf, jnp.mean(kf, axis=0, keepdims=True), kmean_scr[pf])

    def f_vt():
        vtmp_ref[...] = seg(4)
        vt = vtmp_ref[...].T.astype(BF16)
        for hh in range(MOBA_HEADS):
            vt_scr[pf * n_blocks + tf, hh * ACC_ROWS:hh * ACC_ROWS + HEAD_DIM, :] = vt[_head_rows(hh), :]

    def f_z(i):
        w = D_MIX // 4
        z_buf[pw, :, i * w:(i + 1) * w] = seg(6, i * w, (i + 1) * w).astype(BF16)

    n_chunks = blk // GMLP_CHUNK
    gm = {}
    lane_c = lax.broadcasted_iota(jnp.int32, (GMLP_CHUNK, LANES), 1)
    first_head = lane_c < HEAD_DIM

    def chunk_rows(c):
        return slice(c * GMLP_CHUNK, (c + 1) * GMLP_CHUNK)

    def gm_a():
        for c in range(n_chunks):
            vc = jax.nn.gelu(v_scr[chunk_rows(c), :])
            gm[c] = (vc, _dot(vc.astype(BF16), avg_ref[...]))

    def gm_b():
        for c in range(n_chunks):
            vc, mu = gm[c]
            dv = vc - mu
            gm[c] = (dv, _dot((dv * dv).astype(BF16), avg_ref[...]))

    def gm_c():
        vns = []
        for c in range(n_chunks):
            dv, var = gm[c]
            vns.append((dv * lax.rsqrt(var + EPS) * lng_ref[...] + lnb_ref[...]).astype(BF16))
        zero = jnp.zeros((), BF16)
        outs = []
        for p in range(GMLP_GROUPS // 2):
            vps = [vn[:, p * LANES:(p + 1) * LANES] for vn in vns]
            first = jnp.concatenate([jnp.where(first_head, vp, zero) for vp in vps], axis=1)
            second = jnp.concatenate([jnp.where(first_head, zero, vp) for vp in vps], axis=1)
            outs.append(_dot(ws_ref[p], jnp.concatenate([first, second], axis=0)))
        gm["sv"] = outs

    def gm_d():
        for c in range(n_chunks):
            sv = jnp.concatenate([o[:, c * LANES:(c + 1) * LANES] for o in gm["sv"]], axis=-1) + sbx_ref[...]
            uc = jax.nn.gelu(u_scr[chunk_rows(c), :])
            ya_buf[pw, chunk_rows(c), :] = _rms(uc * sv, bga_ref[...]).astype(BF16)

    ma = {}

    def ma_a():
        for hh in range(MEM_HEADS):
            pair_t = qm_scr[(hh // 2) * LANES:(hh // 2 + 1) * LANES, :]
            rows = lax.broadcasted_iota(jnp.int32, (LANES, blk), 0) < HEAD_DIM
            q_ht = jnp.where(rows if hh % 2 == 0 else ~rows, pair_t, jnp.zeros((), BF16))
            ma[hh] = _dot(km_ref[0, :, _head_lanes(hh)], q_ht)

    def ma_b():
        for hh in range(MEM_HEADS):
            s = ma[hh]
            p = jnp.exp2(s - jnp.max(s, axis=0, keepdims=True))
            o = _dot(vmt_ref[0, _head_rows(hh), :], p.astype(BF16))
            omt_ref[_head_rows(hh), :] = o * (1.0 / jnp.sum(p, axis=0, keepdims=True))

    def ma_c():
        ym_buf[pw] = _rms(omt_ref[...].T, bgm_ref[...]).astype(BF16)

    row_id = lax.broadcasted_iota(jnp.int32, (SEL_ROWS, blk), 0)
    kpos = lax.broadcasted_iota(jnp.int32, (blk, blk), 0)
    qpos = lax.broadcasted_iota(jnp.int32, (blk, blk), 1)
    causal_t = kpos <= qpos
    jp = jnp.maximum(t - 1, 0)

    def qk(j, h):
        return _dot(k_scr[pa * n_blocks + j, :, _head_lanes(h)], qh_ref[h])

    def update(h, s, shift, j):
        sel = sel_ref[h, pl.ds(j, 1), :] > 0.5
        m_old = m_ref[h]
        m_new = jnp.where(sel, jnp.maximum(m_old, jnp.max(s, axis=0, keepdims=True) + shift), m_old)
        alpha = jnp.exp2(m_old - m_new)
        p = jnp.exp2(s - jnp.where(sel, m_new - shift, -NEG))
        m_ref[h] = m_new
        ot_ref[_acc_rows(h), :] = alpha * ot_ref[_acc_rows(h), :] + _dot(
            vt_scr[pa * n_blocks + j, _acc_rows(h), :], p.astype(BF16))

    first_rows = lax.broadcasted_iota(jnp.int32, (LANES, blk), 0) < HEAD_DIM
    for h in range(MOBA_HEADS):
        pair_t = q_buf[pr, (h // 2) * LANES:(h // 2 + 1) * LANES, :]
        qh_ref[h] = jnp.where(first_rows if h % 2 == 0 else ~first_rows, pair_t, jnp.zeros((), BF16))
    km_all = kmean_scr[pa]
    km_hi = km_all.astype(BF16)
    km_lo = (km_all - km_hi.astype(F32)).astype(BF16)
    gs_all = []
    for h in range(MOBA_HEADS):
        keys = jnp.concatenate([k_scr[pa * n_blocks + t, :, _head_lanes(h)],
                                km_hi[:, _head_lanes(h)], km_lo[:, _head_lanes(h)]], axis=0)
        s_ext = _dot(keys, qh_ref[h])
        sb_ref[h] = s_ext[:blk]
        gs_all.append(s_ext[blk:blk + SEL_ROWS] + s_ext[blk + SEL_ROWS:])
    f_u()
    f_v()
    for h in range(MOBA_HEADS):
        gs = jnp.where(row_id < t, gs_all[h], NEG)
        rank = jnp.zeros((SEL_ROWS, blk), jnp.int32)
        for i in range(n_blocks):
            gi = gs[i:i + 1, :]
            ahead = (gi > gs) | ((gi == gs) & (row_id > i))
            rank = rank + ahead.astype(jnp.int32)
        sel_ref[h] = ((rank < MOBA_TOPK) & (row_id < t)).astype(F32)
    gm_a()

    stage1_tasks = {0: [f_q, f_k], 1: [gm_b], 2: [f_vt, f_qm], 3: [gm_c, f_norm_next], 4: [], 5: [gm_d]}
    for h in range(MOBA_HEADS):
        sa_ref[h] = qk(0, h)
        for task in stage1_tasks[h]:
            task()
        s = jnp.where(causal_t, sb_ref[h] + bias_ref[h, 0], NEG)
        m0 = jnp.max(s, axis=0, keepdims=True)
        p = jnp.exp2(s - m0)
        m_ref[h] = m0
        ot_ref[_acc_rows(h), :] = _dot(vt_scr[pa * n_blocks + t, _acc_rows(h), :], p.astype(BF16))

    def far_step(j, cur_ref, nxt_ref):
        order = [("qk", 0), ("qk", 1), ("qk", 2), ("qk", 3), ("pv", 0), ("pv", 1), ("qk", 4), ("qk", 5),
                 ("pv", 2), ("pv", 3), ("pv", 4), ("pv", 5)]
        for kind, h in order:
            if kind == "qk":
                nxt_ref[h] = qk(j + 1, h)
            else:
                update(h, cur_ref[h], cfar_ref[h], j)

    def far_pair(i, carry):
        far_step(2 * i, sa_ref, sb_ref)
        far_step(2 * i + 1, sb_ref, sa_ref)
        return carry

    n_far = jnp.maximum(t - 1, 0)
    lax.fori_loop(0, n_far // 2, far_pair, 0)

    @pl.when(n_far % 2 == 1)
    def _():
        far_step(n_far - 1, sa_ref, sb_ref)

    def prev_step(cur_ref):
        tasks = {0: [lambda: f_z(0), lambda: f_z(1)], 1: [ma_a], 2: [lambda: f_z(2), lambda: f_z(3)],
                 3: [ma_b], 4: [], 5: [ma_c]}
        for h in range(MOBA_HEADS):
            for task in tasks[h]:
                task()
            update(h, cur_ref[h] + bias_ref[h, 1], 0.0, jp)

    @pl.when(jp % 2 == 0)
    def _():
        prev_step(sa_ref)

    @pl.when(jp % 2 == 1)
    def _():
        prev_step(sb_ref)

    heads_t = []
    for h in range(MOBA_HEADS):
        acc = ot_ref[_acc_rows(h), :]
        heads_t.append(acc[:HEAD_DIM] * (1.0 / acc[HEAD_DIM:HEAD_DIM + 1]))
    yb = _rms(jnp.concatenate(heads_t, axis=0).T, bgb_ref[...]).astype(BF16)
    y = jnp.concatenate([ya_buf[pr], yb, ym_buf[pr]], axis=-1).astype(F32)
    y = (y * jax.nn.silu(z_buf[pr].astype(F32))).astype(BF16)
    out = xa_ref[0] + _dot(y, wo_ref[...])
    if final:
        out = _rms(out, fg_ref[...])
    o_ref[0] = out


def _layer(x, km, vmt, g_in, w_in, ln_g, ln_b, w_causal, sb_exp, avg, bga, bgm, bias, cfar, bgb,
           w_out, fg, final):
    b, s, d = x.shape
    m = km.shape[1]
    blk = MOBA_BLOCK
    nb = s // blk
    n_tiles = b * nb
    full = lambda arr: pl.BlockSpec(arr.shape, lambda g: (0,) * arr.ndim)

    def front_tile(g):
        f = jnp.minimum(g, n_tiles - 1)
        return f // nb, f % nb

    def back_tile(g):
        a = jnp.maximum(g - 1, 0)
        return a // nb, a % nb

    x_next = pl.BlockSpec((1, blk, d), lambda g: (*front_tile(g + 1), 0))
    x_back = pl.BlockSpec((1, blk, d), lambda g: (*back_tile(g), 0))
    mem_spec = lambda shape: pl.BlockSpec(shape, lambda g: (front_tile(g)[0], 0, 0))
    vm = lambda shape, dt: pltpu.VMEM(shape, dt)
    return pl.pallas_call(
        functools.partial(_layer_kernel, n_tiles=n_tiles, n_blocks=nb, final=final),
        grid=(n_tiles + 1,),
        in_specs=[x_next, x_back, full(g_in), full(w_in), full(ln_g), full(ln_b), full(w_causal),
                  full(sb_exp), full(avg), full(bga), full(bgm),
                  mem_spec((1, m, D_MEM)), mem_spec((1, D_MEM, m)),
                  full(bias), full(cfar), full(bgb), full(w_out), full(fg)],
        out_specs=x_back,
        out_shape=jax.ShapeDtypeStruct((b, s, d), F32),
        scratch_shapes=[
            vm((2 * nb, blk, D_MOBA), BF16),
            vm((2 * nb, MOBA_HEADS * ACC_ROWS, blk), BF16),
            vm((2, SEL_ROWS, D_MOBA), F32),
            vm((2, D_MOBA, blk), BF16), vm((2, blk, D_GMLP), BF16), vm((2, blk, D_MEM), BF16),
            vm((2, blk, D_MIX), BF16),
            vm((blk, D_GMLP), F32), vm((blk, D_GMLP), F32),
            vm((blk, D_MOBA), F32),
            vm((blk, D_MOBA), F32),
            vm((blk, D_MEM), F32),
            vm((D_MEM, blk), BF16),
            vm((D_MEM, blk), F32),
            vm((MOBA_HEADS, LANES, blk), BF16),
            vm((MOBA_HEADS, SEL_ROWS, blk), F32),
            vm((MOBA_HEADS, 1, blk), F32),
            vm((MOBA_HEADS * ACC_ROWS, blk), F32),
            vm((MOBA_HEADS, blk, blk), F32),
            vm((MOBA_HEADS, blk, blk), F32),
            vm((2, blk, D_MODEL), BF16)],
        compiler_params=_params(1),
        name="layer",
    )(x, x, g_in, w_in, ln_g, ln_b, w_causal, sb_exp, avg, bga, bgm, km, vmt, bias, cfar, bgb, w_out, fg)


def _t5_bucket(dist):
    dist = jnp.maximum(dist, 0)
    max_exact = REL_BUCKETS // 2
    df = jnp.maximum(dist, 1).astype(F32)
    large = max_exact + (jnp.log(df / max_exact) / math.log(REL_MAX_DIST / max_exact)
                         * (REL_BUCKETS - max_exact)).astype(jnp.int32)
    large = jnp.minimum(large, REL_BUCKETS - 1)
    return jnp.where(dist < max_exact, dist, large)


def _bias_tables(rel_bias):
    assert MOBA_BLOCK + 1 >= REL_MAX_DIST
    kq = jnp.arange(MOBA_BLOCK)
    d_own = kq[None, :] - kq[:, None]
    buckets = jnp.stack([_t5_bucket(d_own), _t5_bucket(d_own + MOBA_BLOCK)])
    onehot = (buckets[None] == jnp.arange(REL_BUCKETS)[:, None, None, None]).astype(F32)
    table = rel_bias.astype(F32) * LOG2E
    tiles = jnp.einsum('nh,nxkq->hxkq', table, onehot, precision=lax.Precision.HIGHEST)
    cfar = jnp.broadcast_to(table[REL_BUCKETS - 1][:, None, None], (MOBA_HEADS, 1, MOBA_BLOCK))
    return tiles, cfar


def _pair_spatial(w):
    g, c, _ = w.shape
    return w.reshape(g // 2, 2, c, c).transpose(0, 2, 1, 3).reshape(g // 2, c, 2 * c).astype(BF16)


def kernel(x, mem, norm_g, w_in, gmlp_ln_g, gmlp_ln_b, spatial_w, spatial_b, rel_bias,
           mem_norm_g, w_mem_kv, branch_norm_g, w_out, final_norm_g):
    b, s, d = x.shape
    depth = w_in.shape[0]

    bias, cfar = _bias_tables(rel_bias)
    causal = jnp.tril(jnp.ones((GMLP_CHUNK, GMLP_CHUNK), F32))
    grp = np.arange(D_GMLP) // HEAD_DIM
    avg = jnp.asarray((grp[:, None] == grp[None, :]).astype(np.float32) / HEAD_DIM, BF16)
    fg = final_norm_g.reshape(1, d)

    for l in range(depth):
        km, vmt = _mem_kv(mem, mem_norm_g[l].reshape(1, d), w_mem_kv[l].astype(BF16))
        bg = branch_norm_g[l]
        sb_exp = jnp.repeat(spatial_b[l].T, HEAD_DIM, axis=1)
        x = _layer(x, km, vmt, norm_g[l].reshape(1, d), w_in[l].astype(BF16),
                   gmlp_ln_g[l].reshape(1, D_GMLP), gmlp_ln_b[l].reshape(1, D_GMLP),
                   _pair_spatial(spatial_w[l] * causal[None]), sb_exp, avg,
                   bg[:D_GMLP].reshape(1, D_GMLP), bg[D_GMLP + D_MOBA:].reshape(1, D_MEM),
                   bias, cfar, bg[D_GMLP:D_GMLP + D_MOBA].reshape(1, D_MOBA),
                   w_out[l].astype(BF16), fg, final=(l == depth - 1))
    return x
```

```python
import functools
import math

import numpy as np
import jax
import jax.numpy as jnp
from jax import lax
from jax.experimental import pallas as pl
from jax.experimental.pallas import tpu as pltpu

D_MODEL = 1024
HEAD_DIM = 64
GMLP_GROUPS = 6
MOBA_HEADS = 6
MEM_HEADS = 4
D_GMLP = GMLP_GROUPS * HEAD_DIM
D_MOBA = MOBA_HEADS * HEAD_DIM
D_MEM = MEM_HEADS * HEAD_DIM
D_MIX = D_GMLP + D_MOBA + D_MEM
D_IN = 2 * D_GMLP + 3 * D_MOBA + D_MEM + D_MIX
GMLP_CHUNK = 128
MOBA_BLOCK = 256
MOBA_TOPK = 3
REL_BUCKETS = 32
REL_MAX_DIST = 128
EPS = 1e-6
NEG = -1e30
LOG2E = math.log2(math.e)

LANES = 128
SEL_ROWS = 16
VMEM_LIMIT = 56 * 1024 * 1024

F32 = jnp.float32
BF16 = jnp.bfloat16

_OFF = np.cumsum([0, D_GMLP, D_GMLP, D_MOBA, D_MOBA, D_MOBA, D_MEM, D_MIX])


def _params(n_axes):
    return pltpu.CompilerParams(
        dimension_semantics=("arbitrary",) * n_axes, vmem_limit_bytes=VMEM_LIMIT)


def _rms(x, g):
    return x * lax.rsqrt(jnp.mean(x * x, axis=-1, keepdims=True) + EPS) * g


def _dot(a, b):
    return jnp.dot(a, b, preferred_element_type=F32)


def _dot_nt(a, b):
    return lax.dot_general(a, b, (((1,), (1,)), ((), ())), preferred_element_type=F32)


def _head_lanes(h):
    return slice((h // 2) * LANES, (h // 2 + 1) * LANES)


def _head_rows(h):
    return slice(h * HEAD_DIM, (h + 1) * HEAD_DIM)


ONES_ROWS = 16
ACC_ROWS = HEAD_DIM + ONES_ROWS


def _acc_rows(h):
    return slice(h * ACC_ROWS, (h + 1) * ACC_ROWS)


def _isolate_head(pair, h):
    lane = lax.broadcasted_iota(jnp.int32, pair.shape, 1)
    keep = (lane < HEAD_DIM) if h % 2 == 0 else (lane >= HEAD_DIM)
    return jnp.where(keep, pair, jnp.zeros((), pair.dtype))


def _mem_kv_kernel(mem_ref, g_ref, w_ref, km_ref, vmt_ref):
    mn = _rms(mem_ref[0], g_ref[...]).astype(BF16)
    kv = _dot(mn, w_ref[...])
    km_ref[0] = kv[:, :D_MEM].astype(BF16)
    vmt_ref[0] = kv[:, D_MEM:].T.astype(BF16)


def _mem_kv(mem, g, w):
    b, m, d = mem.shape
    return pl.pallas_call(
        _mem_kv_kernel,
        grid=(b,),
        in_specs=[pl.BlockSpec((1, m, d), lambda i: (i, 0, 0)),
                  pl.BlockSpec((1, d), lambda i: (0, 0)),
                  pl.BlockSpec((d, 2 * D_MEM), lambda i: (0, 0))],
        out_specs=[pl.BlockSpec((1, m, D_MEM), lambda i: (i, 0, 0)),
                   pl.BlockSpec((1, D_MEM, m), lambda i: (i, 0, 0))],
        out_shape=[jax.ShapeDtypeStruct((b, m, D_MEM), BF16),
                   jax.ShapeDtypeStruct((b, D_MEM, m), BF16)],
        compiler_params=_params(1),
        name="mem_kv",
    )(mem, g, w)


def _layer_kernel(xn_ref, xa_ref, g_ref, w_ref, lng_ref, lnb_ref, ws_ref, sbx_ref, avg_ref, bga_ref,
                  bgm_ref, km_ref, vmt_ref, bias_ref, cfar_ref, bgb_ref, wo_ref, fg_ref,
                  o_ref,
                  k_scr, vt_scr, kmean_scr,
                  q_buf, ya_buf, ym_buf, z_buf,
                  u_scr, v_scr, vtmp_ref, qtmp_ref, qmtmp_ref, qm_scr, omt_ref,
                  qh_ref, sel_ref, m_ref, ot_ref, sa_ref, sb_ref, h_buf,
                  *, n_tiles, n_blocks, final):
    g = pl.program_id(0)
    blk = MOBA_BLOCK
    f = jnp.minimum(g, n_tiles - 1)
    a = jnp.maximum(g - 1, 0)
    tf = f % n_blocks
    pf = (f // n_blocks) % 2
    t = a % n_blocks
    pa = (a // n_blocks) % 2
    pw = g % 2
    pr = 1 - pw

    @pl.when(g == 0)
    def _():
        q_buf[1] = jnp.zeros((D_MOBA, blk), BF16)
        ya_buf[1] = jnp.zeros((blk, D_GMLP), BF16)
        ym_buf[1] = jnp.zeros((blk, D_MEM), BF16)
        z_buf[1] = jnp.zeros((blk, D_MIX), BF16)
        kmean_scr[...] = jnp.zeros_like(kmean_scr)
        k_scr[0] = jnp.zeros((blk, D_MOBA), BF16)
        vt_scr[...] = jnp.ones(vt_scr.shape, BF16)
        h_buf[1] = _rms(xa_ref[0], g_ref[...]).astype(BF16)

    def f_norm_next():
        h_buf[pw] = _rms(xn_ref[0], g_ref[...]).astype(BF16)

    scale = HEAD_DIM ** -0.5 * LOG2E

    def seg(i, lo=0, hi=None):
        c0 = _OFF[i] + lo
        c1 = _OFF[i + 1] if hi is None else _OFF[i] + hi
        return _dot(h_buf[pr], w_ref[:, c0:c1])

    def f_u():
        u_scr[...] = seg(0)

    def f_v():
        v_scr[...] = seg(1)

    def f_qm():
        qmtmp_ref[...] = seg(5) * scale
        qm_scr[...] = qmtmp_ref[...].T.astype(BF16)

    def f_q():
        qtmp_ref[...] = seg(2) * scale
        q_buf[pw] = qtmp_ref[...].T.astype(BF16)

    def f_k():
        kf = seg(3)
        k_scr[pf * n_blocks + tf] = kf.astype(BF16)
        mean_rows = lax.broadcasted_iota(jnp.int32, (SEL_ROWS, D_MOBA), 0)
        kmean_scr[pf] = jnp.where(mean_rows == tf, jnp.mean(kf, axis=0, keepdims=True), kmean_scr[pf])

    def f_vt():
        vtmp_ref[...] = seg(4)
        vt = vtmp_ref[...].T.astype(BF16)
        for hh in range(MOBA_HEADS):
            vt_scr[pf * n_blocks + tf, hh * ACC_ROWS:hh * ACC_ROWS + HEAD_DIM, :] = vt[_head_rows(hh), :]

    def f_z(i):
        w = D_MIX // 4
        z_buf[pw, :, i * w:(i + 1) * w] = seg(6, i * w, (i + 1) * w).astype(BF16)

    n_chunks = blk // GMLP_CHUNK
    gm = {}
    lane_c = lax.broadcasted_iota(jnp.int32, (GMLP_CHUNK, LANES), 1)
    first_head = lane_c < HEAD_DIM

    def chunk_rows(c):
        return slice(c * GMLP_CHUNK, (c + 1) * GMLP_CHUNK)

    def gm_a():
        for c in range(n_chunks):
            vc = jax.nn.gelu(v_scr[chunk_rows(c), :])
            gm[c] = (vc, _dot(vc.astype(BF16), avg_ref[...]))

    def gm_b():
        for c in range(n_chunks):
            vc, mu = gm[c]
            dv = vc - mu
            gm[c] = (dv, _dot((dv * dv).astype(BF16), avg_ref[...]))

    def gm_c():
        vns = []
        for c in range(n_chunks):
            dv, var = gm[c]
            vns.append((dv * lax.rsqrt(var + EPS) * lng_ref[...] + lnb_ref[...]).astype(BF16))
        zero = jnp.zeros((), BF16)
        outs = []
        for p in range(GMLP_GROUPS // 2):
            vps = [vn[:, p * LANES:(p + 1) * LANES] for vn in vns]
            first = jnp.concatenate([jnp.where(first_head, vp, zero) for vp in vps], axis=1)
            second = jnp.concatenate([jnp.where(first_head, zero, vp) for vp in vps], axis=1)
            outs.append(_dot(ws_ref[p], jnp.concatenate([first, second], axis=0)))
        gm["sv"] = outs

    def gm_d():
        for c in range(n_chunks):
            sv = jnp.concatenate([o[:, c * LANES:(c + 1) * LANES] for o in gm["sv"]], axis=-1) + sbx_ref[...]
            uc = jax.nn.gelu(u_scr[chunk_rows(c), :])
            ya_buf[pw, chunk_rows(c), :] = _rms(uc * sv, bga_ref[...]).astype(BF16)

    ma = {}

    def ma_a():
        for hh in range(MEM_HEADS):
            pair_t = qm_scr[(hh // 2) * LANES:(hh // 2 + 1) * LANES, :]
            rows = lax.broadcasted_iota(jnp.int32, (LANES, blk), 0) < HEAD_DIM
            q_ht = jnp.where(rows if hh % 2 == 0 else ~rows, pair_t, jnp.zeros((), BF16))
            ma[hh] = _dot(km_ref[0, :, _head_lanes(hh)], q_ht)

    def ma_b():
        for hh in range(MEM_HEADS):
            s = ma[hh]
            p = jnp.exp2(s - jnp.max(s, axis=0, keepdims=True))
            o = _dot(vmt_ref[0, _head_rows(hh), :], p.astype(BF16))
            omt_ref[_head_rows(hh), :] = o * (1.0 / jnp.sum(p, axis=0, keepdims=True))

    def ma_c():
        ym_buf[pw] = _rms(omt_ref[...].T, bgm_ref[...]).astype(BF16)

    row_id = lax.broadcasted_iota(jnp.int32, (SEL_ROWS, blk), 0)
    kpos = lax.broadcasted_iota(jnp.int32, (blk, blk), 0)
    qpos = lax.broadcasted_iota(jnp.int32, (blk, blk), 1)
    causal_t = kpos <= qpos
    jp = jnp.maximum(t - 1, 0)

    def qk(j, h):
        return _dot(k_scr[pa * n_blocks + j, :, _head_lanes(h)], qh_ref[h])

    def update(h, s, shift, j):
        sel = sel_ref[h, pl.ds(j, 1), :] > 0.5
        m_old = m_ref[h]
        m_new = jnp.where(sel, jnp.maximum(m_old, jnp.max(s, axis=0, keepdims=True) + shift), m_old)
        alpha = jnp.exp2(m_old - m_new)
        p = jnp.exp2(s - jnp.where(sel, m_new - shift, -NEG))
        m_ref[h] = m_new
        ot_ref[_acc_rows(h), :] = alpha * ot_ref[_acc_rows(h), :] + _dot(
            vt_scr[pa * n_blocks + j, _acc_rows(h), :], p.astype(BF16))

    first_rows = lax.broadcasted_iota(jnp.int32, (LANES, blk), 0) < HEAD_DIM
    for h in range(MOBA_HEADS):
        pair_t = q_buf[pr, (h // 2) * LANES:(h // 2 + 1) * LANES, :]
        qh_ref[h] = jnp.where(first_rows if h % 2 == 0 else ~first_rows, pair_t, jnp.zeros((), BF16))
    km_all = kmean_scr[pa]
    km_hi = km_all.astype(BF16)
    km_lo = (km_all - km_hi.astype(F32)).astype(BF16)
    gs_all = []
    for h in range(MOBA_HEADS):
        keys = jnp.concatenate([k_scr[pa * n_blocks + t, :, _head_lanes(h)],
                                km_hi[:, _head_lanes(h)], km_lo[:, _head_lanes(h)]], axis=0)
        s_ext = _dot(keys, qh_ref[h])
        sb_ref[h] = s_ext[:blk]
        gs_all.append(s_ext[blk:blk + SEL_ROWS] + s_ext[blk + SEL_ROWS:])
    f_u()
    f_v()
    for h in range(MOBA_HEADS):
        gs = jnp.where(row_id < t, gs_all[h], NEG)
        rank = jnp.zeros((SEL_ROWS, blk), jnp.int32)
        for i in range(n_blocks):
            gi = gs[i:i + 1, :]
            ahead = (gi > gs) | ((gi == gs) & (row_id > i))
            rank = rank + ahead.astype(jnp.int32)
        sel_ref[h] = ((rank < MOBA_TOPK) & (row_id < t)).astype(F32)
    gm_a()

    stage1_tasks = {0: [f_q, f_k], 1: [gm_b], 2: [f_vt, f_qm], 3: [gm_c, f_norm_next], 4: [], 5: [gm_d]}
    for h in range(MOBA_HEADS):
        sa_ref[h] = qk(0, h)
    for h in range(MOBA_HEADS):
        for task in stage1_tasks[h]:
            task()
        s = jnp.where(causal_t, sb_ref[h] + bias_ref[h, 0], NEG)
        m0 = jnp.max(s, axis=0, keepdims=True)
        p = jnp.exp2(s - m0)
        m_ref[h] = m0
        ot_ref[_acc_rows(h), :] = _dot(vt_scr[pa * n_blocks + t, _acc_rows(h), :], p.astype(BF16))

    def far_step(j, cur_ref, nxt_ref):
        order = [("qk", 0), ("qk", 1), ("qk", 2), ("qk", 3), ("pv", 0), ("pv", 1), ("qk", 4), ("qk", 5),
                 ("pv", 2), ("pv", 3), ("pv", 4), ("pv", 5)]
        for kind, h in order:
            if kind == "qk":
                nxt_ref[h] = qk(j + 1, h)
            else:
                update(h, cur_ref[h], cfar_ref[h], j)

    def far_pair(i, carry):
        far_step(2 * i, sa_ref, sb_ref)
        far_step(2 * i + 1, sb_ref, sa_ref)
        return carry

    n_far = jnp.maximum(t - 1, 0)
    lax.fori_loop(0, n_far // 2, far_pair, 0)

    @pl.when(n_far % 2 == 1)
    def _():
        far_step(n_far - 1, sa_ref, sb_ref)

    def prev_step(cur_ref):
        tasks = {0: [ma_a, lambda: f_z(0)], 1: [lambda: f_z(1)], 2: [ma_b, lambda: f_z(2)],
                 3: [lambda: f_z(3)], 4: [ma_c], 5: []}
        for h in range(MOBA_HEADS):
            for task in tasks[h]:
                task()
            update(h, cur_ref[h] + bias_ref[h, 1], 0.0, jp)

    @pl.when(jp % 2 == 0)
    def _():
        prev_step(sa_ref)

    @pl.when(jp % 2 == 1)
    def _():
        prev_step(sb_ref)

    heads_t = []
    for h in range(MOBA_HEADS):
        acc = ot_ref[_acc_rows(h), :]
        heads_t.append(acc[:HEAD_DIM] * (1.0 / acc[HEAD_DIM:HEAD_DIM + 1]))
    yb = _rms(jnp.concatenate(heads_t, axis=0).T, bgb_ref[...]).astype(BF16)
    y = jnp.concatenate([ya_buf[pr], yb, ym_buf[pr]], axis=-1).astype(F32)
    y = (y * jax.nn.silu(z_buf[pr].astype(F32))).astype(BF16)
    out = xa_ref[0] + _dot(y, wo_ref[...])
    if final:
        out = _rms(out, fg_ref[...])
    o_ref[0] = out


def _layer(x, km, vmt, g_in, w_in, ln_g, ln_b, w_causal, sb_exp, avg, bga, bgm, bias, cfar, bgb,
           w_out, fg, final):
    b, s, d = x.shape
    m = km.shape[1]
    blk = MOBA_BLOCK
    nb = s // blk
    n_tiles = b * nb
    full = lambda arr: pl.BlockSpec(arr.shape, lambda g: (0,) * arr.ndim)

    def front_tile(g):
        f = jnp.minimum(g, n_tiles - 1)
        return f // nb, f % nb

    def back_tile(g):
        a = jnp.maximum(g - 1, 0)
        return a // nb, a % nb

    x_next = pl.BlockSpec((1, blk, d), lambda g: (*front_tile(g + 1), 0))
    x_back = pl.BlockSpec((1, blk, d), lambda g: (*back_tile(g), 0))
    mem_spec = lambda shape: pl.BlockSpec(shape, lambda g: (front_tile(g)[0], 0, 0))
    vm = lambda shape, dt: pltpu.VMEM(shape, dt)
    return pl.pallas_call(
        functools.partial(_layer_kernel, n_tiles=n_tiles, n_blocks=nb, final=final),
        grid=(n_tiles + 1,),
        in_specs=[x_next, x_back, full(g_in), full(w_in), full(ln_g), full(ln_b), full(w_causal),
                  full(sb_exp), full(avg), full(bga), full(bgm),
                  mem_spec((1, m, D_MEM)), mem_spec((1, D_MEM, m)),
                  full(bias), full(cfar), full(bgb), full(w_out), full(fg)],
        out_specs=x_back,
        out_shape=jax.ShapeDtypeStruct((b, s, d), F32),
        scratch_shapes=[
            vm((2 * nb, blk, D_MOBA), BF16),
            vm((2 * nb, MOBA_HEADS * ACC_ROWS, blk), BF16),
            vm((2, SEL_ROWS, D_MOBA), F32),
            vm((2, D_MOBA, blk), BF16), vm((2, blk, D_GMLP), BF16), vm((2, blk, D_MEM), BF16),
            vm((2, blk, D_MIX), BF16),
            vm((blk, D_GMLP), F32), vm((blk, D_GMLP), F32),
            vm((blk, D_MOBA), F32),
            vm((blk, D_MOBA), F32),
            vm((blk, D_MEM), F32),
            vm((D_MEM, blk), BF16),
            vm((D_MEM, blk), F32),
            vm((MOBA_HEADS, LANES, blk), BF16),
            vm((MOBA_HEADS, SEL_ROWS, blk), F32),
            vm((MOBA_HEADS, 1, blk), F32),
            vm((MOBA_HEADS * ACC_ROWS, blk), F32),
            vm((MOBA_HEADS, blk, blk), F32),
            vm((MOBA_HEADS, blk, blk), F32),
            vm((2, blk, D_MODEL), BF16)],
        compiler_params=_params(1),
        name="layer",
    )(x, x, g_in, w_in, ln_g, ln_b, w_causal, sb_exp, avg, bga, bgm, km, vmt, bias, cfar, bgb, w_out, fg)


def _t5_bucket(dist):
    dist = np.maximum(dist, 0)
    max_exact = REL_BUCKETS // 2
    df = np.maximum(dist, 1).astype(np.float32)
    large = max_exact + (np.log(df / np.float32(max_exact)) / np.float32(math.log(REL_MAX_DIST / max_exact))
                         * np.float32(REL_BUCKETS - max_exact)).astype(np.int32)
    large = np.minimum(large, REL_BUCKETS - 1)
    return np.where(dist < max_exact, dist, large)


def _bias_tables(rel_bias):
    assert MOBA_BLOCK + 1 >= REL_MAX_DIST
    kq = np.arange(MOBA_BLOCK)
    d_own = kq[None, :] - kq[:, None]
    buckets = np.stack([_t5_bucket(d_own), _t5_bucket(d_own + MOBA_BLOCK)])
    onehot = jnp.asarray(buckets[None] == np.arange(REL_BUCKETS)[:, None, None, None]).astype(F32)
    table = rel_bias.astype(F32) * LOG2E
    tiles = jnp.einsum('nh,nxkq->hxkq', table, onehot, precision=lax.Precision.HIGHEST)
    cfar = jnp.broadcast_to(table[REL_BUCKETS - 1][:, None, None], (MOBA_HEADS, 1, MOBA_BLOCK))
    return tiles, cfar


def _pair_spatial(w):
    g, c, _ = w.shape
    return w.reshape(g // 2, 2, c, c).transpose(0, 2, 1, 3).reshape(g // 2, c, 2 * c).astype(BF16)


def kernel(x, mem, norm_g, w_in, gmlp_ln_g, gmlp_ln_b, spatial_w, spatial_b, rel_bias,
           mem_norm_g, w_mem_kv, branch_norm_g, w_out, final_norm_g):
    b, s, d = x.shape
    depth = w_in.shape[0]

    bias, cfar = _bias_tables(rel_bias)
    causal = jnp.tril(jnp.ones((GMLP_CHUNK, GMLP_CHUNK), F32))
    grp = np.arange(D_GMLP) // HEAD_DIM
    avg = jnp.asarray((grp[:, None] == grp[None, :]).astype(np.float32) / HEAD_DIM, BF16)
    fg = final_norm_g.reshape(1, d)

    for l in range(depth):
        km, vmt = _mem_kv(mem, mem_norm_g[l].reshape(1, d), w_mem_kv[l].astype(BF16))
        bg = branch_norm_g[l]
        sb_exp = jnp.repeat(spatial_b[l].T, HEAD_DIM, axis=1)
        x = _layer(x, km, vmt, norm_g[l].reshape(1, d), w_in[l].astype(BF16),
                   gmlp_ln_g[l].reshape(1, D_GMLP), gmlp_ln_b[l].reshape(1, D_GMLP),
                   _pair_spatial(spatial_w[l] * causal[None]), sb_exp, avg,
                   bg[:D_GMLP].reshape(1, D_GMLP), bg[D_GMLP + D_MOBA:].reshape(1, D_MEM),
                   bias, cfar, bg[D_GMLP:D_GMLP + D_MOBA].reshape(1, D_MOBA),
                   w_out[l].astype(BF16), fg, final=(l == depth - 1))
    return x
```

```python
import functools
import math

import numpy as np
import jax
import jax.numpy as jnp
from jax import lax
from jax.experimental import pallas as pl
from jax.experimental.pallas import tpu as pltpu

D_MODEL = 1024
HEAD_DIM = 64
GMLP_GROUPS = 6
MOBA_HEADS = 6
MEM_HEADS = 4
D_GMLP = GMLP_GROUPS * HEAD_DIM
D_MOBA = MOBA_HEADS * HEAD_DIM
D_MEM = MEM_HEADS * HEAD_DIM
D_MIX = D_GMLP + D_MOBA + D_MEM
D_IN = 2 * D_GMLP + 3 * D_MOBA + D_MEM + D_MIX
GMLP_CHUNK = 128
MOBA_BLOCK = 256
MOBA_TOPK = 3
REL_BUCKETS = 32
REL_MAX_DIST = 128
EPS = 1e-6
NEG = -1e30
LOG2E = math.log2(math.e)

LANES = 128
SEL_ROWS = 16
V7X_VMEM_BYTES = 64 * 1024 * 1024
VMEM_LIMIT = V7X_VMEM_BYTES - 8 * 1024 * 1024

F32 = jnp.float32
BF16 = jnp.bfloat16

_OFF = np.cumsum([0, D_GMLP, D_GMLP, D_MOBA, D_MOBA, D_MOBA, D_MEM, D_MIX])


def _params(n_axes):
    return pltpu.CompilerParams(
        dimension_semantics=("arbitrary",) * n_axes, vmem_limit_bytes=VMEM_LIMIT)


def _rms(x, g):
    return x * lax.rsqrt(jnp.mean(x * x, axis=-1, keepdims=True) + EPS) * g


def _dot(a, b):
    return jnp.dot(a, b, preferred_element_type=F32)


def _head_lanes(h):
    return slice((h // 2) * LANES, (h // 2 + 1) * LANES)


def _head_rows(h):
    return slice(h * HEAD_DIM, (h + 1) * HEAD_DIM)


ONES_ROWS = 16
ACC_ROWS = HEAD_DIM + ONES_ROWS


def _acc_rows(h):
    return slice(h * ACC_ROWS, (h + 1) * ACC_ROWS)


def _isolate_head_t(q_t_ref, h):
    pair_t = q_t_ref[(h // 2) * LANES:(h // 2 + 1) * LANES, :]
    row = lax.broadcasted_iota(jnp.int32, pair_t.shape, 0)
    keep = (row < HEAD_DIM) if h % 2 == 0 else (row >= HEAD_DIM)
    return jnp.where(keep, pair_t, jnp.zeros((), pair_t.dtype))


def _mem_kv_kernel(mem_ref, g_ref, w_ref, km_ref, vmt_ref):
    mn = _rms(mem_ref[0], g_ref[...]).astype(BF16)
    kv = _dot(mn, w_ref[...])
    km_ref[0] = kv[:, :D_MEM].astype(BF16)
    vmt_ref[0] = kv[:, D_MEM:].T.astype(BF16)


def _mem_kv(mem, g, w):
    b, m, d = mem.shape
    return pl.pallas_call(
        _mem_kv_kernel,
        grid=(b,),
        in_specs=[pl.BlockSpec((1, m, d), lambda i: (i, 0, 0)),
                  pl.BlockSpec((1, d), lambda i: (0, 0)),
                  pl.BlockSpec((d, 2 * D_MEM), lambda i: (0, 0))],
        out_specs=[pl.BlockSpec((1, m, D_MEM), lambda i: (i, 0, 0)),
                   pl.BlockSpec((1, D_MEM, m), lambda i: (i, 0, 0))],
        out_shape=[jax.ShapeDtypeStruct((b, m, D_MEM), BF16),
                   jax.ShapeDtypeStruct((b, D_MEM, m), BF16)],
        compiler_params=_params(1),
        name="mem_kv",
    )(mem, g, w)


def _layer_kernel(xn_ref, xa_ref, g_ref, w_ref, lng_ref, lnb_ref, ws_ref, sbx_ref, avg_ref, bga_ref,
                  bgm_ref, km_ref, vmt_ref, bias_ref, cfar_ref, bgb_ref, wo_ref, fg_ref,
                  o_ref,
                  k_scr, vt_scr, kmean_scr,
                  q_buf, ya_buf, ym_buf, z_buf,
                  u_scr, v_scr, vtmp_ref, qtmp_ref, qmtmp_ref, qm_scr, omt_ref,
                  qh_ref, sel_ref, m_ref, ot_ref, sa_ref, sb_ref, h_buf,
                  *, n_tiles, n_blocks, final):
    g = pl.program_id(0)
    blk = MOBA_BLOCK
    f = jnp.minimum(g, n_tiles - 1)
    a = jnp.maximum(g - 1, 0)
    tf = f % n_blocks
    pf = (f // n_blocks) % 2
    t = a % n_blocks
    pa = (a // n_blocks) % 2
    pw = g % 2
    pr = 1 - pw

    @pl.when(g == 0)
    def _():
        q_buf[1] = jnp.zeros((D_MOBA, blk), BF16)
        ya_buf[1] = jnp.zeros((blk, D_GMLP), BF16)
        ym_buf[1] = jnp.zeros((blk, D_MEM), BF16)
        z_buf[1] = jnp.zeros((blk, D_MIX), BF16)
        kmean_scr[...] = jnp.zeros_like(kmean_scr)
        k_scr[0] = jnp.zeros((blk, D_MOBA), BF16)
        vt_scr[...] = jnp.ones(vt_scr.shape, BF16)
        h_buf[1] = _rms(xa_ref[0], g_ref[...]).astype(BF16)

    def f_norm_next():
        h_buf[pw] = _rms(xn_ref[0], g_ref[...]).astype(BF16)

    scale = HEAD_DIM ** -0.5 * LOG2E

    def seg(i, lo=0, hi=None):
        c0 = _OFF[i] + lo
        c1 = _OFF[i + 1] if hi is None else _OFF[i] + hi
        return _dot(h_buf[pr], w_ref[:, c0:c1])

    def f_u():
        u_scr[...] = seg(0)

    def f_v():
        v_scr[...] = seg(1)

    def f_qm():
        qmtmp_ref[...] = seg(5) * scale
        qm_scr[...] = qmtmp_ref[...].T.astype(BF16)

    def f_q():
        qtmp_ref[...] = seg(2) * scale
        q_buf[pw] = qtmp_ref[...].T.astype(BF16)

    def f_k():
        kf = seg(3)
        k_scr[pf * n_blocks + tf] = kf.astype(BF16)
        mean_rows = lax.broadcasted_iota(jnp.int32, (SEL_ROWS, D_MOBA), 0)
        kmean_scr[pf] = jnp.where(mean_rows == tf, jnp.mean(kf, axis=0, keepdims=True), kmean_scr[pf])

    def f_vt():
        vtmp_ref[...] = seg(4)
        vt = vtmp_ref[...].T.astype(BF16)
        for hh in range(MOBA_HEADS):
            vt_scr[pf * n_blocks + tf, hh * ACC_ROWS:hh * ACC_ROWS + HEAD_DIM, :] = vt[_head_rows(hh), :]

    def f_z(i):
        w = D_MIX // 4
        z_buf[pw, :, i * w:(i + 1) * w] = seg(6, i * w, (i + 1) * w).astype(BF16)

    n_chunks = blk // GMLP_CHUNK
    gm = {}
    lane_c = lax.broadcasted_iota(jnp.int32, (GMLP_CHUNK, LANES), 1)
    first_head = lane_c < HEAD_DIM

    def chunk_rows(c):
        return slice(c * GMLP_CHUNK, (c + 1) * GMLP_CHUNK)

    def gm_a():
        for c in range(n_chunks):
            vc = jax.nn.gelu(v_scr[chunk_rows(c), :])
            gm[c] = (vc, _dot(vc.astype(BF16), avg_ref[...]))

    def gm_b():
        for c in range(n_chunks):
            vc, mu = gm[c]
            dv = vc - mu
            gm[c] = (dv, _dot((dv * dv).astype(BF16), avg_ref[...]))

    def gm_c():
        vns = []
        for c in range(n_chunks):
            dv, var = gm[c]
            vns.append((dv * lax.rsqrt(var + EPS) * lng_ref[...] + lnb_ref[...]).astype(BF16))
        zero = jnp.zeros((), BF16)
        outs = []
        for p in range(GMLP_GROUPS // 2):
            vps = [vn[:, p * LANES:(p + 1) * LANES] for vn in vns]
            first = jnp.concatenate([jnp.where(first_head, vp, zero) for vp in vps], axis=1)
            second = jnp.concatenate([jnp.where(first_head, zero, vp) for vp in vps], axis=1)
            outs.append(_dot(ws_ref[p], jnp.concatenate([first, second], axis=0)))
        gm["sv"] = outs

    def gm_d():
        for c in range(n_chunks):
            sv = jnp.concatenate([o[:, c * LANES:(c + 1) * LANES] for o in gm["sv"]], axis=-1) + sbx_ref[...]
            uc = jax.nn.gelu(u_scr[chunk_rows(c), :])
            ya_buf[pw, chunk_rows(c), :] = _rms(uc * sv, bga_ref[...]).astype(BF16)

    ma = {}

    def ma_a():
        for hh in range(MEM_HEADS):
            ma[hh] = _dot(km_ref[0, :, _head_lanes(hh)], _isolate_head_t(qm_scr, hh))

    def ma_b():
        for hh in range(MEM_HEADS):
            s = ma[hh]
            p = jnp.exp2(s - jnp.max(s, axis=0, keepdims=True))
            o = _dot(vmt_ref[0, _head_rows(hh), :], p.astype(BF16))
            omt_ref[_head_rows(hh), :] = o * (1.0 / jnp.sum(p, axis=0, keepdims=True))

    def ma_c():
        ym_buf[pw] = _rms(omt_ref[...].T, bgm_ref[...]).astype(BF16)

    row_id = lax.broadcasted_iota(jnp.int32, (SEL_ROWS, blk), 0)
    kpos = lax.broadcasted_iota(jnp.int32, (blk, blk), 0)
    qpos = lax.broadcasted_iota(jnp.int32, (blk, blk), 1)
    causal_t = kpos <= qpos
    jp = jnp.maximum(t - 1, 0)

    def qk(j, h):
        return _dot(k_scr[pa * n_blocks + j, :, _head_lanes(h)], qh_ref[h])

    def update(h, s, shift, j):
        sel = sel_ref[h, pl.ds(j, 1), :] > 0.5
        m_old = m_ref[h]
        m_new = jnp.where(sel, jnp.maximum(m_old, jnp.max(s, axis=0, keepdims=True) + shift), m_old)
        alpha = jnp.exp2(m_old - m_new)
        p = jnp.exp2(s - jnp.where(sel, m_new - shift, -NEG))
        m_ref[h] = m_new
        ot_ref[_acc_rows(h), :] = alpha * ot_ref[_acc_rows(h), :] + _dot(
            vt_scr[pa * n_blocks + j, _acc_rows(h), :], p.astype(BF16))

    for h in range(MOBA_HEADS):
        qh_ref[h] = _isolate_head_t(q_buf.at[pr], h)
    km_all = kmean_scr[pa]
    km_hi = km_all.astype(BF16)
    km_lo = (km_all - km_hi.astype(F32)).astype(BF16)
    gs_all = []
    for h in range(MOBA_HEADS):
        keys = jnp.concatenate([k_scr[pa * n_blocks + t, :, _head_lanes(h)],
                                km_hi[:, _head_lanes(h)], km_lo[:, _head_lanes(h)]], axis=0)
        s_ext = _dot(keys, qh_ref[h])
        sb_ref[h] = s_ext[:blk]
        gs_all.append(s_ext[blk:blk + SEL_ROWS] + s_ext[blk + SEL_ROWS:])
    f_u()
    f_v()
    for h in range(MOBA_HEADS):
        gs = jnp.where(row_id < t, gs_all[h], NEG)
        rank = jnp.zeros((SEL_ROWS, blk), jnp.int32)
        for i in range(n_blocks):
            gi = gs[i:i + 1, :]
            ahead = (gi > gs) | ((gi == gs) & (row_id > i))
            rank = rank + ahead.astype(jnp.int32)
        sel_ref[h] = ((rank < MOBA_TOPK) & (row_id < t)).astype(F32)
    gm_a()

    stage1_tasks = {0: [f_q, f_k], 1: [gm_b], 2: [f_vt, f_qm], 3: [gm_c, f_norm_next], 4: [], 5: [gm_d]}
    for h in range(MOBA_HEADS):
        sa_ref[h] = qk(0, h)
    for h in range(MOBA_HEADS):
        for task in stage1_tasks[h]:
            task()
        s = jnp.where(causal_t, sb_ref[h] + bias_ref[h, 0], NEG)
        m0 = jnp.max(s, axis=0, keepdims=True)
        p = jnp.exp2(s - m0)
        m_ref[h] = m0
        ot_ref[_acc_rows(h), :] = _dot(vt_scr[pa * n_blocks + t, _acc_rows(h), :], p.astype(BF16))

    def far_step(j, cur_ref, nxt_ref):
        order = [("qk", 0), ("qk", 1), ("qk", 2), ("qk", 3), ("pv", 0), ("pv", 1), ("qk", 4), ("qk", 5),
                 ("pv", 2), ("pv", 3), ("pv", 4), ("pv", 5)]
        for kind, h in order:
            if kind == "qk":
                nxt_ref[h] = qk(j + 1, h)
            else:
                update(h, cur_ref[h], cfar_ref[h], j)

    def far_pair(i, carry):
        far_step(2 * i, sa_ref, sb_ref)
        far_step(2 * i + 1, sb_ref, sa_ref)
        return carry

    n_far = jnp.maximum(t - 1, 0)
    lax.fori_loop(0, n_far // 2, far_pair, 0)

    @pl.when(n_far % 2 == 1)
    def _():
        far_step(n_far - 1, sa_ref, sb_ref)

    def prev_step(cur_ref):
        tasks = {0: [ma_a, lambda: f_z(0)], 1: [lambda: f_z(1)], 2: [ma_b, lambda: f_z(2)],
                 3: [lambda: f_z(3)], 4: [ma_c], 5: []}
        for h in range(MOBA_HEADS):
            for task in tasks[h]:
                task()
            update(h, cur_ref[h] + bias_ref[h, 1], 0.0, jp)

    @pl.when(jp % 2 == 0)
    def _():
        prev_step(sa_ref)

    @pl.when(jp % 2 == 1)
    def _():
        prev_step(sb_ref)

    heads_t = []
    for h in range(MOBA_HEADS):
        acc = ot_ref[_acc_rows(h), :]
        heads_t.append(acc[:HEAD_DIM] * (1.0 / acc[HEAD_DIM:HEAD_DIM + 1]))
    yb = _rms(jnp.concatenate(heads_t, axis=0).T, bgb_ref[...]).astype(BF16)
    y = jnp.concatenate([ya_buf[pr], yb, ym_buf[pr]], axis=-1).astype(F32)
    y = (y * jax.nn.silu(z_buf[pr].astype(F32))).astype(BF16)
    out = xa_ref[0] + _dot(y, wo_ref[...])
    if final:
        out = _rms(out, fg_ref[...])
    o_ref[0] = out


def _layer(x, km, vmt, g_in, w_in, ln_g, ln_b, w_causal, sb_exp, avg, bga, bgm, bias, cfar, bgb,
           w_out, fg, final):
    b, s, d = x.shape
    m = km.shape[1]
    blk = MOBA_BLOCK
    nb = s // blk
    n_tiles = b * nb
    full = lambda arr: pl.BlockSpec(arr.shape, lambda g: (0,) * arr.ndim)

    def front_tile(g):
        f = jnp.minimum(g, n_tiles - 1)
        return f // nb, f % nb

    def back_tile(g):
        a = jnp.maximum(g - 1, 0)
        return a // nb, a % nb

    x_next = pl.BlockSpec((1, blk, d), lambda g: (*front_tile(g + 1), 0))
    x_back = pl.BlockSpec((1, blk, d), lambda g: (*back_tile(g), 0))
    mem_spec = lambda shape: pl.BlockSpec(shape, lambda g: (front_tile(g)[0], 0, 0))
    vm = lambda shape, dt: pltpu.VMEM(shape, dt)
    return pl.pallas_call(
        functools.partial(_layer_kernel, n_tiles=n_tiles, n_blocks=nb, final=final),
        grid=(n_tiles + 1,),
        in_specs=[x_next, x_back, full(g_in), full(w_in), full(ln_g), full(ln_b), full(w_causal),
                  full(sb_exp), full(avg), full(bga), full(bgm),
                  mem_spec((1, m, D_MEM)), mem_spec((1, D_MEM, m)),
                  full(bias), full(cfar), full(bgb), full(w_out), full(fg)],
        out_specs=x_back,
        out_shape=jax.ShapeDtypeStruct((b, s, d), F32),
        scratch_shapes=[
            vm((2 * nb, blk, D_MOBA), BF16),
            vm((2 * nb, MOBA_HEADS * ACC_ROWS, blk), BF16),
            vm((2, SEL_ROWS, D_MOBA), F32),
            vm((2, D_MOBA, blk), BF16), vm((2, blk, D_GMLP), BF16), vm((2, blk, D_MEM), BF16),
            vm((2, blk, D_MIX), BF16),
            vm((blk, D_GMLP), F32), vm((blk, D_GMLP), F32),
            vm((blk, D_MOBA), F32),
            vm((blk, D_MOBA), F32),
            vm((blk, D_MEM), F32),
            vm((D_MEM, blk), BF16),
            vm((D_MEM, blk), F32),
            vm((MOBA_HEADS, LANES, blk), BF16),
            vm((MOBA_HEADS, SEL_ROWS, blk), F32),
            vm((MOBA_HEADS, 1, blk), F32),
            vm((MOBA_HEADS * ACC_ROWS, blk), F32),
            vm((MOBA_HEADS, blk, blk), F32),
            vm((MOBA_HEADS, blk, blk), F32),
            vm((2, blk, D_MODEL), BF16)],
        compiler_params=_params(1),
        name="layer",
    )(x, x, g_in, w_in, ln_g, ln_b, w_causal, sb_exp, avg, bga, bgm, km, vmt, bias, cfar, bgb, w_out, fg)


def _t5_bucket(dist):
    dist = np.maximum(dist, 0)
    max_exact = REL_BUCKETS // 2
    df = np.maximum(dist, 1).astype(np.float32)
    large = max_exact + (np.log(df / np.float32(max_exact)) / np.float32(math.log(REL_MAX_DIST / max_exact))
                         * np.float32(REL_BUCKETS - max_exact)).astype(np.int32)
    large = np.minimum(large, REL_BUCKETS - 1)
    return np.where(dist < max_exact, dist, large)


def _bias_tables(rel_bias):
    assert MOBA_BLOCK + 1 >= REL_MAX_DIST
    kq = np.arange(MOBA_BLOCK)
    d_own = kq[None, :] - kq[:, None]
    buckets = np.stack([_t5_bucket(d_own), _t5_bucket(d_own + MOBA_BLOCK)])
    onehot = jnp.asarray(buckets[None] == np.arange(REL_BUCKETS)[:, None, None, None]).astype(F32)
    table = rel_bias.astype(F32) * LOG2E
    tiles = jnp.einsum('nh,nxkq->hxkq', table, onehot, precision=lax.Precision.HIGHEST)
    cfar = jnp.broadcast_to(table[REL_BUCKETS - 1][:, None, None], (MOBA_HEADS, 1, MOBA_BLOCK))
    return tiles, cfar


def _pair_spatial(w):
    g, c, _ = w.shape
    return w.reshape(g // 2, 2, c, c).transpose(0, 2, 1, 3).reshape(g // 2, c, 2 * c).astype(BF16)


def kernel(x, mem, norm_g, w_in, gmlp_ln_g, gmlp_ln_b, spatial_w, spatial_b, rel_bias,
           mem_norm_g, w_mem_kv, branch_norm_g, w_out, final_norm_g):
    b, s, d = x.shape
    depth = w_in.shape[0]

    bias, cfar = _bias_tables(rel_bias)
    causal = jnp.tril(jnp.ones((GMLP_CHUNK, GMLP_CHUNK), F32))
    grp = np.arange(D_GMLP) // HEAD_DIM
    avg = jnp.asarray((grp[:, None] == grp[None, :]).astype(np.float32) / HEAD_DIM, BF16)
    fg = final_norm_g.reshape(1, d)

    for l in range(depth):
        km, vmt = _mem_kv(mem, mem_norm_g[l].reshape(1, d), w_mem_kv[l].astype(BF16))
        bg = branch_norm_g[l]
        sb_exp = jnp.repeat(spatial_b[l].T, HEAD_DIM, axis=1)
        x = _layer(x, km, vmt, norm_g[l].reshape(1, d), w_in[l].astype(BF16),
                   gmlp_ln_g[l].reshape(1, D_GMLP), gmlp_ln_b[l].reshape(1, D_GMLP),
                   _pair_spatial(spatial_w[l] * causal[None]), sb_exp, avg,
                   bg[:D_GMLP].reshape(1, D_GMLP), bg[D_GMLP + D_MOBA:].reshape(1, D_MEM),
                   bias, cfar, bg[D_GMLP:D_GMLP + D_MOBA].reshape(1, D_MOBA),
                   w_out[l].astype(BF16), fg, final=(l == depth - 1))
    return x
```

```python
import functools
import math

import numpy as np
import jax
import jax.numpy as jnp
from jax import lax
from jax.experimental import pallas as pl
from jax.experimental.pallas import tpu as pltpu

D_MODEL = 1024
HEAD_DIM = 64
GMLP_GROUPS = 6
MOBA_HEADS = 6
MEM_HEADS = 4
D_GMLP = GMLP_GROUPS * HEAD_DIM
D_MOBA = MOBA_HEADS * HEAD_DIM
D_MEM = MEM_HEADS * HEAD_DIM
D_MIX = D_GMLP + D_MOBA + D_MEM
D_IN = 2 * D_GMLP + 3 * D_MOBA + D_MEM + D_MIX
GMLP_CHUNK = 128
MOBA_BLOCK = 256
MOBA_TOPK = 3
REL_BUCKETS = 32
REL_MAX_DIST = 128
EPS = 1e-6
NEG = -1e30
LOG2E = math.log2(math.e)

LANES = 128
SEL_ROWS = 16
V7X_VMEM_BYTES = 64 * 1024 * 1024
VMEM_LIMIT = V7X_VMEM_BYTES - 8 * 1024 * 1024

F32 = jnp.float32
BF16 = jnp.bfloat16

_OFF = np.cumsum([0, D_GMLP, D_GMLP, D_MOBA, D_MOBA, D_MOBA, D_MEM, D_MIX])


def _params(n_axes):
    return pltpu.CompilerParams(
        dimension_semantics=("arbitrary",) * n_axes, vmem_limit_bytes=VMEM_LIMIT)


def _rms(x, g):
    return x * lax.rsqrt(jnp.mean(x * x, axis=-1, keepdims=True) + EPS) * g


def _dot(a, b):
    return jnp.dot(a, b, preferred_element_type=F32)


def _head_lanes(h):
    return slice((h // 2) * LANES, (h // 2 + 1) * LANES)


def _head_rows(h):
    return slice(h * HEAD_DIM, (h + 1) * HEAD_DIM)


ONES_ROWS = 16
ACC_ROWS = HEAD_DIM + ONES_ROWS


def _acc_rows(h):
    return slice(h * ACC_ROWS, (h + 1) * ACC_ROWS)


def _isolate_head_t(q_t_ref, h):
    pair_t = q_t_ref[(h // 2) * LANES:(h // 2 + 1) * LANES, :]
    row = lax.broadcasted_iota(jnp.int32, pair_t.shape, 0)
    keep = (row < HEAD_DIM) if h % 2 == 0 else (row >= HEAD_DIM)
    return jnp.where(keep, pair_t, jnp.zeros((), pair_t.dtype))


def _mem_kv_kernel(mem_ref, g_ref, w_ref, km_ref, vmt_ref):
    mn = _rms(mem_ref[0], g_ref[...]).astype(BF16)
    kv = _dot(mn, w_ref[...])
    km_ref[0] = kv[:, :D_MEM].astype(BF16)
    vmt_ref[0] = kv[:, D_MEM:].T.astype(BF16)


def _mem_kv(mem, g, w):
    b, m, d = mem.shape
    return pl.pallas_call(
        _mem_kv_kernel,
        grid=(b,),
        in_specs=[pl.BlockSpec((1, m, d), lambda i: (i, 0, 0)),
                  pl.BlockSpec((1, d), lambda i: (0, 0)),
                  pl.BlockSpec((d, 2 * D_MEM), lambda i: (0, 0))],
        out_specs=[pl.BlockSpec((1, m, D_MEM), lambda i: (i, 0, 0)),
                   pl.BlockSpec((1, D_MEM, m), lambda i: (i, 0, 0))],
        out_shape=[jax.ShapeDtypeStruct((b, m, D_MEM), BF16),
                   jax.ShapeDtypeStruct((b, D_MEM, m), BF16)],
        compiler_params=_params(1),
        name="mem_kv",
    )(mem, g, w)


def _layer_kernel(xn_ref, xa_ref, vec_ref, w_ref, ws_ref, sbx_ref, avg_ref,
                  km_ref, vmt_ref, bias_ref, cfar_ref, wo_ref,
                  o_ref,
                  k_scr, vt_scr, kmean_scr,
                  q_buf, ya_buf, ym_buf, z_buf,
                  u_scr, v_scr, vtmp_ref, qtmp_ref, qmtmp_ref, qm_scr, omt_ref,
                  qh_ref, sel_ref, m_ref, ot_ref, sa_ref, sb_ref, h_buf,
                  *, n_tiles, n_blocks, final):
    g = pl.program_id(0)
    blk = MOBA_BLOCK
    g_in, ln_g, ln_b = vec_ref[0:1, :], vec_ref[1:2, :D_GMLP], vec_ref[2:3, :D_GMLP]
    bg_a, bg_m, bg_b = vec_ref[3:4, :D_GMLP], vec_ref[4:5, :D_MEM], vec_ref[5:6, :D_MOBA]
    g_final = vec_ref[6:7, :]
    f = jnp.minimum(g, n_tiles - 1)
    a = jnp.maximum(g - 1, 0)
    tf = f % n_blocks
    pf = (f // n_blocks) % 2
    t = a % n_blocks
    pa = (a // n_blocks) % 2
    pw = g % 2
    pr = 1 - pw

    @pl.when(g == 0)
    def _():
        q_buf[1] = jnp.zeros((D_MOBA, blk), BF16)
        ya_buf[1] = jnp.zeros((blk, D_GMLP), BF16)
        ym_buf[1] = jnp.zeros((blk, D_MEM), BF16)
        z_buf[1] = jnp.zeros((blk, D_MIX), BF16)
        kmean_scr[...] = jnp.zeros_like(kmean_scr)
        k_scr[0] = jnp.zeros((blk, D_MOBA), BF16)
        vt_scr[...] = jnp.ones(vt_scr.shape, BF16)
        h_buf[1] = _rms(xa_ref[0], g_in).astype(BF16)

    def f_norm_next():
        h_buf[pw] = _rms(xn_ref[0], g_in).astype(BF16)

    scale = HEAD_DIM ** -0.5 * LOG2E

    def seg(i, lo=0, hi=None):
        c0 = _OFF[i] + lo
        c1 = _OFF[i + 1] if hi is None else _OFF[i] + hi
        return _dot(h_buf[pr], w_ref[:, c0:c1])

    def f_u():
        u_scr[...] = seg(0)

    def f_v():
        v_scr[...] = seg(1)

    def f_qm():
        qmtmp_ref[...] = seg(5) * scale
        qm_scr[...] = qmtmp_ref[...].T.astype(BF16)

    def f_q():
        qtmp_ref[...] = seg(2) * scale
        q_buf[pw] = qtmp_ref[...].T.astype(BF16)

    def f_k():
        kf = seg(3)
        k_scr[pf * n_blocks + tf] = kf.astype(BF16)
        mean_rows = lax.broadcasted_iota(jnp.int32, (SEL_ROWS, D_MOBA), 0)
        kmean_scr[pf] = jnp.where(mean_rows == tf, jnp.mean(kf, axis=0, keepdims=True), kmean_scr[pf])

    def f_vt():
        vtmp_ref[...] = seg(4)
        vt = vtmp_ref[...].T.astype(BF16)
        for hh in range(MOBA_HEADS):
            vt_scr[pf * n_blocks + tf, hh * ACC_ROWS:hh * ACC_ROWS + HEAD_DIM, :] = vt[_head_rows(hh), :]

    def f_z(i):
        w = D_MIX // 4
        z_buf[pw, :, i * w:(i + 1) * w] = seg(6, i * w, (i + 1) * w).astype(BF16)

    n_chunks = blk // GMLP_CHUNK
    gm = {}
    lane_c = lax.broadcasted_iota(jnp.int32, (GMLP_CHUNK, LANES), 1)
    first_head = lane_c < HEAD_DIM

    def chunk_rows(c):
        return slice(c * GMLP_CHUNK, (c + 1) * GMLP_CHUNK)

    def gm_a():
        for c in range(n_chunks):
            vc = jax.nn.gelu(v_scr[chunk_rows(c), :])
            gm[c] = (vc, _dot(vc.astype(BF16), avg_ref[...]))

    def gm_b():
        for c in range(n_chunks):
            vc, mu = gm[c]
            dv = vc - mu
            gm[c] = (dv, _dot((dv * dv).astype(BF16), avg_ref[...]))

    def gm_c():
        vns = []
        for c in range(n_chunks):
            dv, var = gm[c]
            vns.append((dv * lax.rsqrt(var + EPS) * ln_g + ln_b).astype(BF16))
        zero = jnp.zeros((), BF16)
        outs = []
        for p in range(GMLP_GROUPS // 2):
            vps = [vn[:, p * LANES:(p + 1) * LANES] for vn in vns]
            first = jnp.concatenate([jnp.where(first_head, vp, zero) for vp in vps], axis=1)
            second = jnp.concatenate([jnp.where(first_head, zero, vp) for vp in vps], axis=1)
            outs.append(_dot(ws_ref[p], jnp.concatenate([first, second], axis=0)))
        gm["sv"] = outs

    def gm_d():
        for c in range(n_chunks):
            sv = jnp.concatenate([o[:, c * LANES:(c + 1) * LANES] for o in gm["sv"]], axis=-1) + sbx_ref[...]
            uc = jax.nn.gelu(u_scr[chunk_rows(c), :])
            ya_buf[pw, chunk_rows(c), :] = _rms(uc * sv, bg_a).astype(BF16)

    ma = {}

    def ma_a():
        for hh in range(MEM_HEADS):
            ma[hh] = _dot(km_ref[0, :, _head_lanes(hh)], _isolate_head_t(qm_scr, hh))

    def ma_b():
        for hh in range(MEM_HEADS):
            s = ma[hh]
            p = jnp.exp2(s - jnp.max(s, axis=0, keepdims=True))
            o = _dot(vmt_ref[0, _head_rows(hh), :], p.astype(BF16))
            omt_ref[_head_rows(hh), :] = o * (1.0 / jnp.sum(p, axis=0, keepdims=True))

    def ma_c():
        ym_buf[pw] = _rms(omt_ref[...].T, bg_m).astype(BF16)

    row_id = lax.broadcasted_iota(jnp.int32, (SEL_ROWS, blk), 0)
    kpos = lax.broadcasted_iota(jnp.int32, (blk, blk), 0)
    qpos = lax.broadcasted_iota(jnp.int32, (blk, blk), 1)
    causal_t = kpos <= qpos
    jp = jnp.maximum(t - 1, 0)

    def qk(j, h):
        return _dot(k_scr[pa * n_blocks + j, :, _head_lanes(h)], qh_ref[h])

    def update(h, s, shift, j):
        sel = sel_ref[h, pl.ds(j, 1), :] > 0.5
        m_old = m_ref[h]
        m_new = jnp.where(sel, jnp.maximum(m_old, jnp.max(s, axis=0, keepdims=True) + shift), m_old)
        alpha = jnp.exp2(m_old - m_new)
        p = jnp.exp2(s - jnp.where(sel, m_new - shift, -NEG))
        m_ref[h] = m_new
        ot_ref[_acc_rows(h), :] = alpha * ot_ref[_acc_rows(h), :] + _dot(
            vt_scr[pa * n_blocks + j, _acc_rows(h), :], p.astype(BF16))

    for h in range(MOBA_HEADS):
        qh_ref[h] = _isolate_head_t(q_buf.at[pr], h)
    km_all = kmean_scr[pa]
    km_hi = km_all.astype(BF16)
    km_lo = (km_all - km_hi.astype(F32)).astype(BF16)
    gs_all = []
    for h in range(MOBA_HEADS):
        keys = jnp.concatenate([k_scr[pa * n_blocks + t, :, _head_lanes(h)],
                                km_hi[:, _head_lanes(h)], km_lo[:, _head_lanes(h)]], axis=0)
        s_ext = _dot(keys, qh_ref[h])
        sb_ref[h] = s_ext[:blk]
        gs_all.append(s_ext[blk:blk + SEL_ROWS] + s_ext[blk + SEL_ROWS:])
    f_u()
    f_v()
    for h in range(MOBA_HEADS):
        gs = jnp.where(row_id < t, gs_all[h], NEG)
        rank = jnp.zeros((SEL_ROWS, blk), jnp.int32)
        for i in range(n_blocks):
            gi = gs[i:i + 1, :]
            ahead = (gi > gs) | ((gi == gs) & (row_id > i))
            rank = rank + ahead.astype(jnp.int32)
        sel_ref[h] = ((rank < MOBA_TOPK) & (row_id < t)).astype(F32)
    gm_a()

    stage1_tasks = {0: [f_q, f_k], 1: [gm_b], 2: [f_vt, f_qm], 3: [gm_c, f_norm_next], 4: [], 5: [gm_d]}
    for h in range(MOBA_HEADS):
        sa_ref[h] = qk(0, h)
    for h in range(MOBA_HEADS):
        for task in stage1_tasks[h]:
            task()
        s = jnp.where(causal_t, sb_ref[h] + bias_ref[h, 0], NEG)
        m0 = jnp.max(s, axis=0, keepdims=True)
        p = jnp.exp2(s - m0)
        m_ref[h] = m0
        ot_ref[_acc_rows(h), :] = _dot(vt_scr[pa * n_blocks + t, _acc_rows(h), :], p.astype(BF16))

    def far_step(j, cur_ref, nxt_ref):
        order = [("qk", 0), ("qk", 1), ("qk", 2), ("qk", 3), ("pv", 0), ("pv", 1), ("qk", 4), ("qk", 5),
                 ("pv", 2), ("pv", 3), ("pv", 4), ("pv", 5)]
        for kind, h in order:
            if kind == "qk":
                nxt_ref[h] = qk(j + 1, h)
            else:
                update(h, cur_ref[h], cfar_ref[h], j)

    def far_pair(i, carry):
        far_step(2 * i, sa_ref, sb_ref)
        far_step(2 * i + 1, sb_ref, sa_ref)
        return carry

    n_far = jnp.maximum(t - 1, 0)
    lax.fori_loop(0, n_far // 2, far_pair, 0)

    @pl.when(n_far % 2 == 1)
    def _():
        far_step(n_far - 1, sa_ref, sb_ref)

    def prev_step(cur_ref):
        tasks = {0: [ma_a, lambda: f_z(0)], 1: [lambda: f_z(1)], 2: [ma_b, lambda: f_z(2)],
                 3: [lambda: f_z(3)], 4: [ma_c], 5: []}
        for h in range(MOBA_HEADS):
            for task in tasks[h]:
                task()
            update(h, cur_ref[h] + bias_ref[h, 1], 0.0, jp)

    @pl.when(jp % 2 == 0)
    def _():
        prev_step(sa_ref)

    @pl.when(jp % 2 == 1)
    def _():
        prev_step(sb_ref)

    heads_t = []
    for h in range(MOBA_HEADS):
        acc = ot_ref[_acc_rows(h), :]
        heads_t.append(acc[:HEAD_DIM] * (1.0 / acc[HEAD_DIM:HEAD_DIM + 1]))
    yb = _rms(jnp.concatenate(heads_t, axis=0).T, bg_b).astype(BF16)
    y = jnp.concatenate([ya_buf[pr], yb, ym_buf[pr]], axis=-1).astype(F32)
    y = (y * jax.nn.silu(z_buf[pr].astype(F32))).astype(BF16)
    out = xa_ref[0] + _dot(y, wo_ref[...])
    if final:
        out = _rms(out, g_final)
    o_ref[0] = out


def _layer(x, km, vmt, vec, w_in, w_causal, sb_exp, avg, bias, cfar, w_out, final):
    b, s, d = x.shape
    m = km.shape[1]
    blk = MOBA_BLOCK
    nb = s // blk
    n_tiles = b * nb
    full = lambda arr: pl.BlockSpec(arr.shape, lambda g: (0,) * arr.ndim)

    def front_tile(g):
        f = jnp.minimum(g, n_tiles - 1)
        return f // nb, f % nb

    def back_tile(g):
        a = jnp.maximum(g - 1, 0)
        return a // nb, a % nb

    x_next = pl.BlockSpec((1, blk, d), lambda g: (*front_tile(g + 1), 0))
    x_back = pl.BlockSpec((1, blk, d), lambda g: (*back_tile(g), 0))
    mem_spec = lambda shape: pl.BlockSpec(shape, lambda g: (front_tile(g)[0], 0, 0))
    vm = lambda shape, dt: pltpu.VMEM(shape, dt)
    return pl.pallas_call(
        functools.partial(_layer_kernel, n_tiles=n_tiles, n_blocks=nb, final=final),
        grid=(n_tiles + 1,),
        in_specs=[x_next, x_back, full(vec), full(w_in), full(w_causal), full(sb_exp), full(avg),
                  mem_spec((1, m, D_MEM)), mem_spec((1, D_MEM, m)),
                  full(bias), full(cfar), full(w_out)],
        out_specs=x_back,
        out_shape=jax.ShapeDtypeStruct((b, s, d), F32),
        scratch_shapes=[
            vm((2 * nb, blk, D_MOBA), BF16),
            vm((2 * nb, MOBA_HEADS * ACC_ROWS, blk), BF16),
            vm((2, SEL_ROWS, D_MOBA), F32),
            vm((2, D_MOBA, blk), BF16), vm((2, blk, D_GMLP), BF16), vm((2, blk, D_MEM), BF16),
            vm((2, blk, D_MIX), BF16),
            vm((blk, D_GMLP), F32), vm((blk, D_GMLP), F32),
            vm((blk, D_MOBA), F32),
            vm((blk, D_MOBA), F32),
            vm((blk, D_MEM), F32),
            vm((D_MEM, blk), BF16),
            vm((D_MEM, blk), F32),
            vm((MOBA_HEADS, LANES, blk), BF16),
            vm((MOBA_HEADS, SEL_ROWS, blk), F32),
            vm((MOBA_HEADS, 1, blk), F32),
            vm((MOBA_HEADS * ACC_ROWS, blk), F32),
            vm((MOBA_HEADS, blk, blk), F32),
            vm((MOBA_HEADS, blk, blk), F32),
            vm((2, blk, D_MODEL), BF16)],
        compiler_params=_params(1),
        name="layer",
    )(x, x, vec, w_in, w_causal, sb_exp, avg, km, vmt, bias, cfar, w_out)


def _t5_bucket(dist):
    dist = np.maximum(dist, 0)
    max_exact = REL_BUCKETS // 2
    df = np.maximum(dist, 1).astype(np.float32)
    large = max_exact + (np.log(df / np.float32(max_exact)) / np.float32(math.log(REL_MAX_DIST / max_exact))
                         * np.float32(REL_BUCKETS - max_exact)).astype(np.int32)
    large = np.minimum(large, REL_BUCKETS - 1)
    return np.where(dist < max_exact, dist, large)


def _bias_tables(rel_bias):
    assert MOBA_BLOCK + 1 >= REL_MAX_DIST
    kq = np.arange(MOBA_BLOCK)
    d_own = kq[None, :] - kq[:, None]
    buckets = np.stack([_t5_bucket(d_own), _t5_bucket(d_own + MOBA_BLOCK)])
    onehot = jnp.asarray(buckets[None] == np.arange(REL_BUCKETS)[:, None, None, None]).astype(F32)
    table = rel_bias.astype(F32) * LOG2E
    tiles = jnp.einsum('nh,nxkq->hxkq', table, onehot, precision=lax.Precision.HIGHEST)
    cfar = jnp.broadcast_to(table[REL_BUCKETS - 1][:, None, None], (MOBA_HEADS, 1, MOBA_BLOCK))
    return tiles, cfar


def _pack_rows(vectors, width):
    rows = [jnp.pad(v.astype(F32), (0, width - v.shape[0])) for v in vectors]
    rows += [jnp.zeros((width,), F32)] * (8 - len(rows))
    return jnp.stack(rows)


def _pair_spatial(w):
    g, c, _ = w.shape
    return w.reshape(g // 2, 2, c, c).transpose(0, 2, 1, 3).reshape(g // 2, c, 2 * c).astype(BF16)


def kernel(x, mem, norm_g, w_in, gmlp_ln_g, gmlp_ln_b, spatial_w, spatial_b, rel_bias,
           mem_norm_g, w_mem_kv, branch_norm_g, w_out, final_norm_g):
    b, s, d = x.shape
    depth = w_in.shape[0]

    bias, cfar = _bias_tables(rel_bias)
    causal = jnp.tril(jnp.ones((GMLP_CHUNK, GMLP_CHUNK), F32))
    grp = np.arange(D_GMLP) // HEAD_DIM
    avg = jnp.asarray((grp[:, None] == grp[None, :]).astype(np.float32) / HEAD_DIM, BF16)

    for l in range(depth):
        km, vmt = _mem_kv(mem, mem_norm_g[l].reshape(1, d), w_mem_kv[l].astype(BF16))
        bg = branch_norm_g[l]
        sb_exp = jnp.repeat(spatial_b[l].T, HEAD_DIM, axis=1)
        vec = _pack_rows([norm_g[l], gmlp_ln_g[l], gmlp_ln_b[l], bg[:D_GMLP], bg[D_GMLP + D_MOBA:],
                          bg[D_GMLP:D_GMLP + D_MOBA], final_norm_g], d)
        x = _layer(x, km, vmt, vec, w_in[l].astype(BF16), _pair_spatial(spatial_w[l] * causal[None]),
                   sb_exp, avg, bias, cfar, w_out[l].astype(BF16), final=(l == depth - 1))
    return x
```

```python
import functools
import math

import numpy as np
import jax
import jax.numpy as jnp
from jax import lax
from jax.experimental import pallas as pl
from jax.experimental.pallas import tpu as pltpu

D_MODEL = 1024
HEAD_DIM = 64
GMLP_GROUPS = 6
MOBA_HEADS = 6
MEM_HEADS = 4
D_GMLP = GMLP_GROUPS * HEAD_DIM
D_MOBA = MOBA_HEADS * HEAD_DIM
D_MEM = MEM_HEADS * HEAD_DIM
D_MIX = D_GMLP + D_MOBA + D_MEM
D_IN = 2 * D_GMLP + 3 * D_MOBA + D_MEM + D_MIX
GMLP_CHUNK = 128
MOBA_BLOCK = 256
MOBA_TOPK = 3
REL_BUCKETS = 32
REL_MAX_DIST = 128
EPS = 1e-6
NEG = -1e30
LOG2E = math.log2(math.e)

LANES = 128
SEL_ROWS = 16
V7X_VMEM_BYTES = 64 * 1024 * 1024
VMEM_LIMIT = V7X_VMEM_BYTES - 8 * 1024 * 1024

F32 = jnp.float32
BF16 = jnp.bfloat16

_OFF = np.cumsum([0, D_GMLP, D_GMLP, D_MOBA, D_MOBA, D_MOBA, D_MEM, D_MIX])


def _params(n_axes):
    return pltpu.CompilerParams(
        dimension_semantics=("arbitrary",) * n_axes, vmem_limit_bytes=VMEM_LIMIT)


def _rms(x, g):
    return x * lax.rsqrt(jnp.mean(x * x, axis=-1, keepdims=True) + EPS) * g


def _dot(a, b):
    return jnp.dot(a, b, preferred_element_type=F32)


def _head_lanes(h):
    return slice((h // 2) * LANES, (h // 2 + 1) * LANES)


def _head_rows(h):
    return slice(h * HEAD_DIM, (h + 1) * HEAD_DIM)


ONES_ROWS = 16
ACC_ROWS = HEAD_DIM + ONES_ROWS


def _acc_rows(h):
    return slice(h * ACC_ROWS, (h + 1) * ACC_ROWS)


def _isolate_head_t(q_t_ref, h):
    pair_t = q_t_ref[(h // 2) * LANES:(h // 2 + 1) * LANES, :]
    row = lax.broadcasted_iota(jnp.int32, pair_t.shape, 0)
    keep = (row < HEAD_DIM) if h % 2 == 0 else (row >= HEAD_DIM)
    return jnp.where(keep, pair_t, jnp.zeros((), pair_t.dtype))


MEM_KV_BATCH = 4


def _mem_kv_kernel(mem_ref, g_ref, w_ref, km_ref, vmt_ref):
    bt, m, d = mem_ref.shape
    mn = _rms(mem_ref[...].reshape(bt * m, d), g_ref[0]).astype(BF16)
    kv = _dot(mn, w_ref[0])
    for i in range(bt):
        rows = slice(i * m, (i + 1) * m)
        km_ref[0, i] = kv[rows, :D_MEM].astype(BF16)
        vmt_ref[0, i] = kv[rows, D_MEM:].T.astype(BF16)


def _mem_kv(mem, g, w):
    b, m, d = mem.shape
    depth = w.shape[0]
    bt = MEM_KV_BATCH if b % MEM_KV_BATCH == 0 else 1
    return pl.pallas_call(
        _mem_kv_kernel,
        grid=(depth, b // bt),
        in_specs=[pl.BlockSpec((bt, m, d), lambda l, i: (i, 0, 0)),
                  pl.BlockSpec((1, 1, d), lambda l, i: (l, 0, 0)),
                  pl.BlockSpec((1, d, 2 * D_MEM), lambda l, i: (l, 0, 0))],
        out_specs=[pl.BlockSpec((1, bt, m, D_MEM), lambda l, i: (l, i, 0, 0)),
                   pl.BlockSpec((1, bt, D_MEM, m), lambda l, i: (l, i, 0, 0))],
        out_shape=[jax.ShapeDtypeStruct((depth, b, m, D_MEM), BF16),
                   jax.ShapeDtypeStruct((depth, b, D_MEM, m), BF16)],
        compiler_params=_params(2),
        name="mem_kv",
    )(mem, g, w)


def _layer_kernel(xn_ref, xa_ref, vec_ref, w_ref, ws_ref, sbx_ref, avg_ref,
                  km_ref, vmt_ref, bias_ref, cfar_ref, wo_ref,
                  o_ref,
                  k_scr, vt_scr, kmean_scr,
                  q_buf, ya_buf, ym_buf, z_buf,
                  u_scr, v_scr, vtmp_ref, qtmp_ref, qmtmp_ref, qm_scr, omt_ref,
                  qh_ref, sel_ref, m_ref, ot_ref, sa_ref, sb_ref, h_buf,
                  *, n_tiles, n_blocks, final):
    g = pl.program_id(0)
    blk = MOBA_BLOCK
    g_in, ln_g, ln_b = vec_ref[0:1, :], vec_ref[1:2, :D_GMLP], vec_ref[2:3, :D_GMLP]
    bg_a, bg_m, bg_b = vec_ref[3:4, :D_GMLP], vec_ref[4:5, :D_MEM], vec_ref[5:6, :D_MOBA]
    g_final = vec_ref[6:7, :]
    f = jnp.minimum(g, n_tiles - 1)
    a = jnp.maximum(g - 1, 0)
    tf = f % n_blocks
    pf = (f // n_blocks) % 2
    t = a % n_blocks
    pa = (a // n_blocks) % 2
    pw = g % 2
    pr = 1 - pw

    @pl.when(g == 0)
    def _():
        q_buf[1] = jnp.zeros((D_MOBA, blk), BF16)
        ya_buf[1] = jnp.zeros((blk, D_GMLP), BF16)
        ym_buf[1] = jnp.zeros((blk, D_MEM), BF16)
        z_buf[1] = jnp.zeros((blk, D_MIX), BF16)
        kmean_scr[...] = jnp.zeros_like(kmean_scr)
        k_scr[0] = jnp.zeros((blk, D_MOBA), BF16)
        vt_scr[...] = jnp.ones(vt_scr.shape, BF16)
        h_buf[1] = _rms(xa_ref[0], g_in).astype(BF16)

    def f_norm_next():
        h_buf[pw] = _rms(xn_ref[0], g_in).astype(BF16)

    scale = HEAD_DIM ** -0.5 * LOG2E

    def seg(i, lo=0, hi=None):
        c0 = _OFF[i] + lo
        c1 = _OFF[i + 1] if hi is None else _OFF[i] + hi
        return _dot(h_buf[pr], w_ref[:, c0:c1])

    def f_u():
        u_scr[...] = seg(0)

    def f_v():
        v_scr[...] = seg(1)

    def f_qm():
        qmtmp_ref[...] = seg(5) * scale
        qm_scr[...] = qmtmp_ref[...].T.astype(BF16)

    def f_q():
        qtmp_ref[...] = seg(2) * scale
        q_buf[pw] = qtmp_ref[...].T.astype(BF16)

    def f_k():
        kf = seg(3)
        k_scr[pf * n_blocks + tf] = kf.astype(BF16)
        mean_rows = lax.broadcasted_iota(jnp.int32, (SEL_ROWS, D_MOBA), 0)
        kmean_scr[pf] = jnp.where(mean_rows == tf, jnp.mean(kf, axis=0, keepdims=True), kmean_scr[pf])

    def f_vt():
        vtmp_ref[...] = seg(4)
        vt = vtmp_ref[...].T.astype(BF16)
        for hh in range(MOBA_HEADS):
            vt_scr[pf * n_blocks + tf, hh * ACC_ROWS:hh * ACC_ROWS + HEAD_DIM, :] = vt[_head_rows(hh), :]

    def f_z(i):
        w = D_MIX // 4
        z_buf[pw, :, i * w:(i + 1) * w] = seg(6, i * w, (i + 1) * w).astype(BF16)

    n_chunks = blk // GMLP_CHUNK
    gm = {}
    lane_c = lax.broadcasted_iota(jnp.int32, (GMLP_CHUNK, LANES), 1)
    first_head = lane_c < HEAD_DIM

    def chunk_rows(c):
        return slice(c * GMLP_CHUNK, (c + 1) * GMLP_CHUNK)

    def gm_a():
        for c in range(n_chunks):
            vc = jax.nn.gelu(v_scr[chunk_rows(c), :])
            gm[c] = (vc, _dot(vc.astype(BF16), avg_ref[...]))

    def gm_b():
        for c in range(n_chunks):
            vc, mu = gm[c]
            dv = vc - mu
            gm[c] = (dv, _dot((dv * dv).astype(BF16), avg_ref[...]))

    def gm_c():
        vns = []
        for c in range(n_chunks):
            dv, var = gm[c]
            vns.append((dv * lax.rsqrt(var + EPS) * ln_g + ln_b).astype(BF16))
        zero = jnp.zeros((), BF16)
        outs = []
        for p in range(GMLP_GROUPS // 2):
            vps = [vn[:, p * LANES:(p + 1) * LANES] for vn in vns]
            first = jnp.concatenate([jnp.where(first_head, vp, zero) for vp in vps], axis=1)
            second = jnp.concatenate([jnp.where(first_head, zero, vp) for vp in vps], axis=1)
            outs.append(_dot(ws_ref[p], jnp.concatenate([first, second], axis=0)))
        gm["sv"] = outs

    def gm_d():
        for c in range(n_chunks):
            sv = jnp.concatenate([o[:, c * LANES:(c + 1) * LANES] for o in gm["sv"]], axis=-1) + sbx_ref[...]
            uc = jax.nn.gelu(u_scr[chunk_rows(c), :])
            ya_buf[pw, chunk_rows(c), :] = _rms(uc * sv, bg_a).astype(BF16)

    ma = {}

    def ma_a():
        for hh in range(MEM_HEADS):
            ma[hh] = _dot(km_ref[0, :, _head_lanes(hh)], _isolate_head_t(qm_scr, hh))

    def ma_b():
        for hh in range(MEM_HEADS):
            s = ma[hh]
            p = jnp.exp2(s - jnp.max(s, axis=0, keepdims=True))
            o = _dot(vmt_ref[0, _head_rows(hh), :], p.astype(BF16))
            omt_ref[_head_rows(hh), :] = o * (1.0 / jnp.sum(p, axis=0, keepdims=True))

    def ma_c():
        ym_buf[pw] = _rms(omt_ref[...].T, bg_m).astype(BF16)

    row_id = lax.broadcasted_iota(jnp.int32, (SEL_ROWS, blk), 0)
    kpos = lax.broadcasted_iota(jnp.int32, (blk, blk), 0)
    qpos = lax.broadcasted_iota(jnp.int32, (blk, blk), 1)
    causal_t = kpos <= qpos
    jp = jnp.maximum(t - 1, 0)

    def qk(j, h):
        return _dot(k_scr[pa * n_blocks + j, :, _head_lanes(h)], qh_ref[h])

    def update(h, s, shift, j):
        sel = sel_ref[h, pl.ds(j, 1), :] > 0.5
        m_old = m_ref[h]
        m_new = jnp.where(sel, jnp.maximum(m_old, jnp.max(s, axis=0, keepdims=True) + shift), m_old)
        alpha = jnp.exp2(m_old - m_new)
        p = jnp.exp2(s - jnp.where(sel, m_new - shift, -NEG))
        m_ref[h] = m_new
        ot_ref[_acc_rows(h), :] = alpha * ot_ref[_acc_rows(h), :] + _dot(
            vt_scr[pa * n_blocks + j, _acc_rows(h), :], p.astype(BF16))

    for h in range(MOBA_HEADS):
        qh_ref[h] = _isolate_head_t(q_buf.at[pr], h)
    km_all = kmean_scr[pa]
    km_hi = km_all.astype(BF16)
    km_lo = (km_all - km_hi.astype(F32)).astype(BF16)
    gs_all = []
    for h in range(MOBA_HEADS):
        keys = jnp.concatenate([k_scr[pa * n_blocks + t, :, _head_lanes(h)],
                                km_hi[:, _head_lanes(h)], km_lo[:, _head_lanes(h)]], axis=0)
        s_ext = _dot(keys, qh_ref[h])
        sb_ref[h] = s_ext[:blk]
        gs_all.append(s_ext[blk:blk + SEL_ROWS] + s_ext[blk + SEL_ROWS:])
    f_u()
    f_v()
    for h in range(MOBA_HEADS):
        gs = jnp.where(row_id < t, gs_all[h], NEG)
        rank = jnp.zeros((SEL_ROWS, blk), jnp.int32)
        for i in range(n_blocks):
            gi = gs[i:i + 1, :]
            ahead = (gi > gs) | ((gi == gs) & (row_id > i))
            rank = rank + ahead.astype(jnp.int32)
        sel_ref[h] = ((rank < MOBA_TOPK) & (row_id < t)).astype(F32)
    gm_a()

    stage1_tasks = {0: [f_q, f_k], 1: [gm_b], 2: [f_vt, f_qm], 3: [gm_c, f_norm_next], 4: [], 5: [gm_d]}
    for h in range(MOBA_HEADS):
        sa_ref[h] = qk(0, h)
    for h in range(MOBA_HEADS):
        for task in stage1_tasks[h]:
            task()
        s = jnp.where(causal_t, sb_ref[h] + bias_ref[h, 0], NEG)
        m0 = jnp.max(s, axis=0, keepdims=True)
        p = jnp.exp2(s - m0)
        m_ref[h] = m0
        ot_ref[_acc_rows(h), :] = _dot(vt_scr[pa * n_blocks + t, _acc_rows(h), :], p.astype(BF16))

    def far_step(j, cur_ref, nxt_ref):
        order = [("qk", 0), ("qk", 1), ("qk", 2), ("qk", 3), ("pv", 0), ("pv", 1), ("qk", 4), ("qk", 5),
                 ("pv", 2), ("pv", 3), ("pv", 4), ("pv", 5)]
        for kind, h in order:
            if kind == "qk":
                nxt_ref[h] = qk(j + 1, h)
            else:
                update(h, cur_ref[h], cfar_ref[h], j)

    def far_pair(i, carry):
        far_step(2 * i, sa_ref, sb_ref)
        far_step(2 * i + 1, sb_ref, sa_ref)
        return carry

    n_far = jnp.maximum(t - 1, 0)
    lax.fori_loop(0, n_far // 2, far_pair, 0)

    @pl.when(n_far % 2 == 1)
    def _():
        far_step(n_far - 1, sa_ref, sb_ref)

    def prev_step(cur_ref):
        tasks = {0: [ma_a, lambda: f_z(0)], 1: [lambda: f_z(1)], 2: [ma_b, lambda: f_z(2)],
                 3: [lambda: f_z(3)], 4: [ma_c], 5: []}
        for h in range(MOBA_HEADS):
            for task in tasks[h]:
                task()
            update(h, cur_ref[h] + bias_ref[h, 1], 0.0, jp)

    @pl.when(jp % 2 == 0)
    def _():
        prev_step(sa_ref)

    @pl.when(jp % 2 == 1)
    def _():
        prev_step(sb_ref)

    heads_t = []
    for h in range(MOBA_HEADS):
        acc = ot_ref[_acc_rows(h), :]
        heads_t.append(acc[:HEAD_DIM] * (1.0 / acc[HEAD_DIM:HEAD_DIM + 1]))
    yb = _rms(jnp.concatenate(heads_t, axis=0).T, bg_b).astype(BF16)
    y = jnp.concatenate([ya_buf[pr], yb, ym_buf[pr]], axis=-1).astype(F32)
    y = (y * jax.nn.silu(z_buf[pr].astype(F32))).astype(BF16)
    out = xa_ref[0] + _dot(y, wo_ref[...])
    if final:
        out = _rms(out, g_final)
    o_ref[0] = out


def _layer(x, km, vmt, vec, w_in, w_causal, sb_exp, avg, bias, cfar, w_out, final):
    b, s, d = x.shape
    m = km.shape[1]
    blk = MOBA_BLOCK
    nb = s // blk
    n_tiles = b * nb
    full = lambda arr: pl.BlockSpec(arr.shape, lambda g: (0,) * arr.ndim)

    def front_tile(g):
        f = jnp.minimum(g, n_tiles - 1)
        return f // nb, f % nb

    def back_tile(g):
        a = jnp.maximum(g - 1, 0)
        return a // nb, a % nb

    x_next = pl.BlockSpec((1, blk, d), lambda g: (*front_tile(g + 1), 0))
    x_back = pl.BlockSpec((1, blk, d), lambda g: (*back_tile(g), 0))
    mem_spec = lambda shape: pl.BlockSpec(shape, lambda g: (front_tile(g)[0], 0, 0))
    vm = lambda shape, dt: pltpu.VMEM(shape, dt)
    return pl.pallas_call(
        functools.partial(_layer_kernel, n_tiles=n_tiles, n_blocks=nb, final=final),
        grid=(n_tiles + 1,),
        in_specs=[x_next, x_back, full(vec), full(w_in), full(w_causal), full(sb_exp), full(avg),
                  mem_spec((1, m, D_MEM)), mem_spec((1, D_MEM, m)),
                  full(bias), full(cfar), full(w_out)],
        out_specs=x_back,
        out_shape=jax.ShapeDtypeStruct((b, s, d), F32),
        scratch_shapes=[
            vm((2 * nb, blk, D_MOBA), BF16),
            vm((2 * nb, MOBA_HEADS * ACC_ROWS, blk), BF16),
            vm((2, SEL_ROWS, D_MOBA), F32),
            vm((2, D_MOBA, blk), BF16), vm((2, blk, D_GMLP), BF16), vm((2, blk, D_MEM), BF16),
            vm((2, blk, D_MIX), BF16),
            vm((blk, D_GMLP), F32), vm((blk, D_GMLP), F32),
            vm((blk, D_MOBA), F32),
            vm((blk, D_MOBA), F32),
            vm((blk, D_MEM), F32),
            vm((D_MEM, blk), BF16),
            vm((D_MEM, blk), F32),
            vm((MOBA_HEADS, LANES, blk), BF16),
            vm((MOBA_HEADS, SEL_ROWS, blk), F32),
            vm((MOBA_HEADS, 1, blk), F32),
            vm((MOBA_HEADS * ACC_ROWS, blk), F32),
            vm((MOBA_HEADS, blk, blk), F32),
            vm((MOBA_HEADS, blk, blk), F32),
            vm((2, blk, D_MODEL), BF16)],
        compiler_params=_params(1),
        name="layer",
    )(x, x, vec, w_in, w_causal, sb_exp, avg, km, vmt, bias, cfar, w_out)


def _t5_bucket(dist):
    dist = np.maximum(dist, 0)
    max_exact = REL_BUCKETS // 2
    df = np.maximum(dist, 1).astype(np.float32)
    large = max_exact + (np.log(df / np.float32(max_exact)) / np.float32(math.log(REL_MAX_DIST / max_exact))
                         * np.float32(REL_BUCKETS - max_exact)).astype(np.int32)
    large = np.minimum(large, REL_BUCKETS - 1)
    return np.where(dist < max_exact, dist, large)


def _bias_tables(rel_bias):
    assert MOBA_BLOCK + 1 >= REL_MAX_DIST
    kq = np.arange(MOBA_BLOCK)
    d_own = kq[None, :] - kq[:, None]
    buckets = np.stack([_t5_bucket(d_own), _t5_bucket(d_own + MOBA_BLOCK)])
    onehot = jnp.asarray(buckets[None] == np.arange(REL_BUCKETS)[:, None, None, None]).astype(F32)
    table = rel_bias.astype(F32) * LOG2E
    tiles = jnp.einsum('nh,nxkq->hxkq', table, onehot, precision=lax.Precision.HIGHEST)
    cfar = jnp.broadcast_to(table[REL_BUCKETS - 1][:, None, None], (MOBA_HEADS, 1, MOBA_BLOCK))
    return tiles, cfar


def _pack_rows(vectors, width):
    rows = [jnp.pad(v.astype(F32), (0, width - v.shape[0])) for v in vectors]
    rows += [jnp.zeros((width,), F32)] * (8 - len(rows))
    return jnp.stack(rows)


def _pair_spatial(w):
    g, c, _ = w.shape
    return w.reshape(g // 2, 2, c, c).transpose(0, 2, 1, 3).reshape(g // 2, c, 2 * c).astype(BF16)


def kernel(x, mem, norm_g, w_in, gmlp_ln_g, gmlp_ln_b, spatial_w, spatial_b, rel_bias,
           mem_norm_g, w_mem_kv, branch_norm_g, w_out, final_norm_g):
    b, s, d = x.shape
    depth = w_in.shape[0]

    bias, cfar = _bias_tables(rel_bias)
    causal = jnp.tril(jnp.ones((GMLP_CHUNK, GMLP_CHUNK), F32))
    grp = np.arange(D_GMLP) // HEAD_DIM
    avg = jnp.asarray((grp[:, None] == grp[None, :]).astype(np.float32) / HEAD_DIM, BF16)

    km_all, vmt_all = _mem_kv(mem, mem_norm_g.reshape(depth, 1, d), w_mem_kv.astype(BF16))
    for l in range(depth):
        km, vmt = km_all[l], vmt_all[l]
        bg = branch_norm_g[l]
        sb_exp = jnp.repeat(spatial_b[l].T, HEAD_DIM, axis=1)
        vec = _pack_rows([norm_g[l], gmlp_ln_g[l], gmlp_ln_b[l], bg[:D_GMLP], bg[D_GMLP + D_MOBA:],
                          bg[D_GMLP:D_GMLP + D_MOBA], final_norm_g], d)
        x = _layer(x, km, vmt, vec, w_in[l].astype(BF16), _pair_spatial(spatial_w[l] * causal[None]),
                   sb_exp, avg, bias, cfar, w_out[l].astype(BF16), final=(l == depth - 1))
    return x
```

```python
import functools
import math

import numpy as np
import jax
import jax.numpy as jnp
from jax import lax
from jax.experimental import pallas as pl
from jax.experimental.pallas import tpu as pltpu

D_MODEL = 1024
HEAD_DIM = 64
GMLP_GROUPS = 6
MOBA_HEADS = 6
MEM_HEADS = 4
D_GMLP = GMLP_GROUPS * HEAD_DIM
D_MOBA = MOBA_HEADS * HEAD_DIM
D_MEM = MEM_HEADS * HEAD_DIM
D_MIX = D_GMLP + D_MOBA + D_MEM
D_IN = 2 * D_GMLP + 3 * D_MOBA + D_MEM + D_MIX
GMLP_CHUNK = 128
MOBA_BLOCK = 256
MOBA_TOPK = 3
REL_BUCKETS = 32
REL_MAX_DIST = 128
EPS = 1e-6
NEG = -1e30
LOG2E = math.log2(math.e)

LANES = 128
SEL_ROWS = 16
V7X_VMEM_BYTES = 64 * 1024 * 1024
VMEM_LIMIT = V7X_VMEM_BYTES - 8 * 1024 * 1024

F32 = jnp.float32
BF16 = jnp.bfloat16

_OFF = np.cumsum([0, D_GMLP, D_GMLP, D_MOBA, D_MOBA, D_MOBA, D_MEM, D_MIX])


def _params(n_axes):
    return pltpu.CompilerParams(
        dimension_semantics=("arbitrary",) * n_axes, vmem_limit_bytes=VMEM_LIMIT)


def _rms(x, g):
    return x * lax.rsqrt(jnp.mean(x * x, axis=-1, keepdims=True) + EPS) * g


def _dot(a, b):
    return jnp.dot(a, b, preferred_element_type=F32)


def _head_lanes(h):
    return slice((h // 2) * LANES, (h // 2 + 1) * LANES)


def _head_rows(h):
    return slice(h * HEAD_DIM, (h + 1) * HEAD_DIM)


ONES_ROWS = 16
ACC_ROWS = HEAD_DIM + ONES_ROWS


def _acc_rows(h):
    return slice(h * ACC_ROWS, (h + 1) * ACC_ROWS)


def _isolate_head_t(q_t_ref, h):
    pair_t = q_t_ref[(h // 2) * LANES:(h // 2 + 1) * LANES, :]
    row = lax.broadcasted_iota(jnp.int32, pair_t.shape, 0)
    keep = (row < HEAD_DIM) if h % 2 == 0 else (row >= HEAD_DIM)
    return jnp.where(keep, pair_t, jnp.zeros((), pair_t.dtype))


MEM_KV_BATCH = 4


def _mem_kv_kernel(mem_ref, g_ref, w_ref, km_ref, vmt_ref):
    bt, m, d = mem_ref.shape
    mn = _rms(mem_ref[...].reshape(bt * m, d), g_ref[0]).astype(BF16)
    kv = _dot(mn, w_ref[0])
    for i in range(bt):
        rows = slice(i * m, (i + 1) * m)
        km_ref[0, i] = kv[rows, :D_MEM].astype(BF16)
        vmt_ref[0, i] = kv[rows, D_MEM:].T.astype(BF16)


def _mem_kv(mem, g, w):
    b, m, d = mem.shape
    depth = w.shape[0]
    bt = MEM_KV_BATCH if b % MEM_KV_BATCH == 0 else 1
    return pl.pallas_call(
        _mem_kv_kernel,
        grid=(depth, b // bt),
        in_specs=[pl.BlockSpec((bt, m, d), lambda l, i: (i, 0, 0)),
                  pl.BlockSpec((1, 1, d), lambda l, i: (l, 0, 0)),
                  pl.BlockSpec((1, d, 2 * D_MEM), lambda l, i: (l, 0, 0))],
        out_specs=[pl.BlockSpec((1, bt, m, D_MEM), lambda l, i: (l, i, 0, 0)),
                   pl.BlockSpec((1, bt, D_MEM, m), lambda l, i: (l, i, 0, 0))],
        out_shape=[jax.ShapeDtypeStruct((depth, b, m, D_MEM), BF16),
                   jax.ShapeDtypeStruct((depth, b, D_MEM, m), BF16)],
        compiler_params=_params(2),
        name="mem_kv",
    )(mem, g, w)


def _layer_kernel(xn_ref, xa_ref, vec_ref, w_ref, ws_ref, sbx_ref, avg_ref,
                  km_ref, vmt_ref, bias_ref, cfar_ref, wo_ref,
                  o_ref,
                  k_scr, vt_scr, kmean_scr,
                  q_buf, ya_buf, ym_buf, z_buf,
                  u_scr, v_scr, vtmp_ref, qtmp_ref, qmtmp_ref, qm_scr, omt_ref,
                  qh_ref, sel_ref, m_ref, ot_ref, sa_ref, sb_ref, h_buf,
                  *, n_tiles, n_blocks, final):
    g = pl.program_id(0)
    blk = MOBA_BLOCK
    g_in, ln_g, ln_b = vec_ref[0:1, :], vec_ref[1:2, :D_GMLP], vec_ref[2:3, :D_GMLP]
    bg_a, bg_m, bg_b = vec_ref[3:4, :D_GMLP], vec_ref[4:5, :D_MEM], vec_ref[5:6, :D_MOBA]
    g_final = vec_ref[6:7, :]
    f = jnp.minimum(g, n_tiles - 1)
    a = jnp.maximum(g - 1, 0)
    tf = f % n_blocks
    pf = (f // n_blocks) % 2
    t = a % n_blocks
    pa = (a // n_blocks) % 2
    pw = g % 2
    pr = 1 - pw

    @pl.when(g == 0)
    def _():
        q_buf[1] = jnp.zeros((D_MOBA, blk), BF16)
        ya_buf[1] = jnp.zeros((blk, D_GMLP), BF16)
        ym_buf[1] = jnp.zeros((blk, D_MEM), BF16)
        z_buf[1] = jnp.zeros((blk, D_MIX), BF16)
        kmean_scr[...] = jnp.zeros_like(kmean_scr)
        k_scr[0] = jnp.zeros((blk, D_MOBA), BF16)
        vt_scr[...] = jnp.ones(vt_scr.shape, BF16)
        h_buf[1] = _rms(xa_ref[0], g_in).astype(BF16)

    def f_norm_next():
        h_buf[pw] = _rms(xn_ref[0], g_in).astype(BF16)

    scale = HEAD_DIM ** -0.5 * LOG2E

    def seg(i, lo=0, hi=None):
        c0 = _OFF[i] + lo
        c1 = _OFF[i + 1] if hi is None else _OFF[i] + hi
        return _dot(h_buf[pr], w_ref[0, :, c0:c1])

    def f_u():
        u_scr[...] = seg(0)

    def f_v():
        v_scr[...] = seg(1)

    def f_qm():
        qmtmp_ref[...] = seg(5) * scale
        qm_scr[...] = qmtmp_ref[...].T.astype(BF16)

    def f_q():
        qtmp_ref[...] = seg(2) * scale
        q_buf[pw] = qtmp_ref[...].T.astype(BF16)

    def f_k():
        kf = seg(3)
        k_scr[pf * n_blocks + tf] = kf.astype(BF16)
        mean_rows = lax.broadcasted_iota(jnp.int32, (SEL_ROWS, D_MOBA), 0)
        kmean_scr[pf] = jnp.where(mean_rows == tf, jnp.mean(kf, axis=0, keepdims=True), kmean_scr[pf])

    def f_vt():
        vtmp_ref[...] = seg(4)
        vt = vtmp_ref[...].T.astype(BF16)
        for hh in range(MOBA_HEADS):
            vt_scr[pf * n_blocks + tf, hh * ACC_ROWS:hh * ACC_ROWS + HEAD_DIM, :] = vt[_head_rows(hh), :]

    def f_z(i):
        w = D_MIX // 4
        z_buf[pw, :, i * w:(i + 1) * w] = seg(6, i * w, (i + 1) * w).astype(BF16)

    n_chunks = blk // GMLP_CHUNK
    gm = {}
    lane_c = lax.broadcasted_iota(jnp.int32, (GMLP_CHUNK, LANES), 1)
    first_head = lane_c < HEAD_DIM

    def chunk_rows(c):
        return slice(c * GMLP_CHUNK, (c + 1) * GMLP_CHUNK)

    def gm_a():
        for c in range(n_chunks):
            vc = jax.nn.gelu(v_scr[chunk_rows(c), :])
            gm[c] = (vc, _dot(vc.astype(BF16), avg_ref[...]))

    def gm_b():
        for c in range(n_chunks):
            vc, mu = gm[c]
            dv = vc - mu
            gm[c] = (dv, _dot((dv * dv).astype(BF16), avg_ref[...]))

    def gm_c():
        vns = []
        for c in range(n_chunks):
            dv, var = gm[c]
            vns.append((dv * lax.rsqrt(var + EPS) * ln_g + ln_b).astype(BF16))
        zero = jnp.zeros((), BF16)
        outs = []
        for p in range(GMLP_GROUPS // 2):
            vps = [vn[:, p * LANES:(p + 1) * LANES] for vn in vns]
            first = jnp.concatenate([jnp.where(first_head, vp, zero) for vp in vps], axis=1)
            second = jnp.concatenate([jnp.where(first_head, zero, vp) for vp in vps], axis=1)
            outs.append(_dot(ws_ref[p], jnp.concatenate([first, second], axis=0)))
        gm["sv"] = outs

    def gm_d():
        for c in range(n_chunks):
            sv = jnp.concatenate([o[:, c * LANES:(c + 1) * LANES] for o in gm["sv"]], axis=-1) + sbx_ref[...]
            uc = jax.nn.gelu(u_scr[chunk_rows(c), :])
            ya_buf[pw, chunk_rows(c), :] = _rms(uc * sv, bg_a).astype(BF16)

    ma = {}

    def ma_a():
        for hh in range(MEM_HEADS):
            ma[hh] = _dot(km_ref[0, 0, :, _head_lanes(hh)], _isolate_head_t(qm_scr, hh))

    def ma_b():
        for hh in range(MEM_HEADS):
            s = ma[hh]
            p = jnp.exp2(s - jnp.max(s, axis=0, keepdims=True))
            o = _dot(vmt_ref[0, 0, _head_rows(hh), :], p.astype(BF16))
            omt_ref[_head_rows(hh), :] = o * (1.0 / jnp.sum(p, axis=0, keepdims=True))

    def ma_c():
        ym_buf[pw] = _rms(omt_ref[...].T, bg_m).astype(BF16)

    row_id = lax.broadcasted_iota(jnp.int32, (SEL_ROWS, blk), 0)
    kpos = lax.broadcasted_iota(jnp.int32, (blk, blk), 0)
    qpos = lax.broadcasted_iota(jnp.int32, (blk, blk), 1)
    causal_t = kpos <= qpos
    jp = jnp.maximum(t - 1, 0)

    def qk(j, h):
        return _dot(k_scr[pa * n_blocks + j, :, _head_lanes(h)], qh_ref[h])

    def update(h, s, shift, j):
        sel = sel_ref[h, pl.ds(j, 1), :] > 0.5
        m_old = m_ref[h]
        m_new = jnp.where(sel, jnp.maximum(m_old, jnp.max(s, axis=0, keepdims=True) + shift), m_old)
        alpha = jnp.exp2(m_old - m_new)
        p = jnp.exp2(s - jnp.where(sel, m_new - shift, -NEG))
        m_ref[h] = m_new
        ot_ref[_acc_rows(h), :] = alpha * ot_ref[_acc_rows(h), :] + _dot(
            vt_scr[pa * n_blocks + j, _acc_rows(h), :], p.astype(BF16))

    for h in range(MOBA_HEADS):
        qh_ref[h] = _isolate_head_t(q_buf.at[pr], h)
    km_all = kmean_scr[pa]
    km_hi = km_all.astype(BF16)
    km_lo = (km_all - km_hi.astype(F32)).astype(BF16)
    gs_all = []
    for h in range(MOBA_HEADS):
        keys = jnp.concatenate([k_scr[pa * n_blocks + t, :, _head_lanes(h)],
                                km_hi[:, _head_lanes(h)], km_lo[:, _head_lanes(h)]], axis=0)
        s_ext = _dot(keys, qh_ref[h])
        sb_ref[h] = s_ext[:blk]
        gs_all.append(s_ext[blk:blk + SEL_ROWS] + s_ext[blk + SEL_ROWS:])
    f_u()
    f_v()
    for h in range(MOBA_HEADS):
        gs = jnp.where(row_id < t, gs_all[h], NEG)
        rank = jnp.zeros((SEL_ROWS, blk), jnp.int32)
        for i in range(n_blocks):
            gi = gs[i:i + 1, :]
            ahead = (gi > gs) | ((gi == gs) & (row_id > i))
            rank = rank + ahead.astype(jnp.int32)
        sel_ref[h] = ((rank < MOBA_TOPK) & (row_id < t)).astype(F32)
    gm_a()

    stage1_tasks = {0: [f_q, f_k], 1: [gm_b], 2: [f_vt, f_qm], 3: [gm_c, f_norm_next], 4: [], 5: [gm_d]}
    for h in range(MOBA_HEADS):
        sa_ref[h] = qk(0, h)
    for h in range(MOBA_HEADS):
        for task in stage1_tasks[h]:
            task()
        s = jnp.where(causal_t, sb_ref[h] + bias_ref[h, 0], NEG)
        m0 = jnp.max(s, axis=0, keepdims=True)
        p = jnp.exp2(s - m0)
        m_ref[h] = m0
        ot_ref[_acc_rows(h), :] = _dot(vt_scr[pa * n_blocks + t, _acc_rows(h), :], p.astype(BF16))

    def far_step(j, cur_ref, nxt_ref):
        order = [("qk", 0), ("qk", 1), ("qk", 2), ("qk", 3), ("pv", 0), ("pv", 1), ("qk", 4), ("qk", 5),
                 ("pv", 2), ("pv", 3), ("pv", 4), ("pv", 5)]
        for kind, h in order:
            if kind == "qk":
                nxt_ref[h] = qk(j + 1, h)
            else:
                update(h, cur_ref[h], cfar_ref[h], j)

    def far_pair(i, carry):
        far_step(2 * i, sa_ref, sb_ref)
        far_step(2 * i + 1, sb_ref, sa_ref)
        return carry

    n_far = jnp.maximum(t - 1, 0)
    lax.fori_loop(0, n_far // 2, far_pair, 0)

    @pl.when(n_far % 2 == 1)
    def _():
        far_step(n_far - 1, sa_ref, sb_ref)

    def prev_step(cur_ref):
        tasks = {0: [ma_a, lambda: f_z(0)], 1: [lambda: f_z(1)], 2: [ma_b, lambda: f_z(2)],
                 3: [lambda: f_z(3)], 4: [ma_c], 5: []}
        for h in range(MOBA_HEADS):
            for task in tasks[h]:
                task()
            update(h, cur_ref[h] + bias_ref[h, 1], 0.0, jp)

    @pl.when(jp % 2 == 0)
    def _():
        prev_step(sa_ref)

    @pl.when(jp % 2 == 1)
    def _():
        prev_step(sb_ref)

    heads_t = []
    for h in range(MOBA_HEADS):
        acc = ot_ref[_acc_rows(h), :]
        heads_t.append(acc[:HEAD_DIM] * (1.0 / acc[HEAD_DIM:HEAD_DIM + 1]))
    yb = _rms(jnp.concatenate(heads_t, axis=0).T, bg_b).astype(BF16)
    y = jnp.concatenate([ya_buf[pr], yb, ym_buf[pr]], axis=-1).astype(F32)
    y = (y * jax.nn.silu(z_buf[pr].astype(F32))).astype(BF16)
    out = xa_ref[0] + _dot(y, wo_ref[0])
    if final:
        out = _rms(out, g_final)
    o_ref[0] = out


def _layer(x, layer, km, vmt, vec, w_in, w_causal, sb_exp, avg, bias, cfar, w_out, final):
    b, s, d = x.shape
    m = km.shape[2]
    blk = MOBA_BLOCK
    nb = s // blk
    n_tiles = b * nb
    full = lambda arr: pl.BlockSpec(arr.shape, lambda g: (0,) * arr.ndim)

    def front_tile(g):
        f = jnp.minimum(g, n_tiles - 1)
        return f // nb, f % nb

    def back_tile(g):
        a = jnp.maximum(g - 1, 0)
        return a // nb, a % nb

    x_next = pl.BlockSpec((1, blk, d), lambda g: (*front_tile(g + 1), 0))
    x_back = pl.BlockSpec((1, blk, d), lambda g: (*back_tile(g), 0))
    of_layer = lambda arr: pl.BlockSpec((1,) + arr.shape[1:], lambda g: (layer,) + (0,) * (arr.ndim - 1))
    mem_spec = lambda shape: pl.BlockSpec(shape, lambda g: (layer, front_tile(g)[0], 0, 0))
    vm = lambda shape, dt: pltpu.VMEM(shape, dt)
    return pl.pallas_call(
        functools.partial(_layer_kernel, n_tiles=n_tiles, n_blocks=nb, final=final),
        grid=(n_tiles + 1,),
        in_specs=[x_next, x_back, full(vec), of_layer(w_in), full(w_causal), full(sb_exp), full(avg),
                  mem_spec((1, 1, m, D_MEM)), mem_spec((1, 1, D_MEM, m)),
                  full(bias), full(cfar), of_layer(w_out)],
        out_specs=x_back,
        out_shape=jax.ShapeDtypeStruct((b, s, d), F32),
        scratch_shapes=[
            vm((2 * nb, blk, D_MOBA), BF16),
            vm((2 * nb, MOBA_HEADS * ACC_ROWS, blk), BF16),
            vm((2, SEL_ROWS, D_MOBA), F32),
            vm((2, D_MOBA, blk), BF16), vm((2, blk, D_GMLP), BF16), vm((2, blk, D_MEM), BF16),
            vm((2, blk, D_MIX), BF16),
            vm((blk, D_GMLP), F32), vm((blk, D_GMLP), F32),
            vm((blk, D_MOBA), F32),
            vm((blk, D_MOBA), F32),
            vm((blk, D_MEM), F32),
            vm((D_MEM, blk), BF16),
            vm((D_MEM, blk), F32),
            vm((MOBA_HEADS, LANES, blk), BF16),
            vm((MOBA_HEADS, SEL_ROWS, blk), F32),
            vm((MOBA_HEADS, 1, blk), F32),
            vm((MOBA_HEADS * ACC_ROWS, blk), F32),
            vm((MOBA_HEADS, blk, blk), F32),
            vm((MOBA_HEADS, blk, blk), F32),
            vm((2, blk, D_MODEL), BF16)],
        compiler_params=_params(1),
        name="layer",
    )(x, x, vec, w_in, w_causal, sb_exp, avg, km, vmt, bias, cfar, w_out)


def _t5_bucket(dist):
    dist = np.maximum(dist, 0)
    max_exact = REL_BUCKETS // 2
    df = np.maximum(dist, 1).astype(np.float32)
    large = max_exact + (np.log(df / np.float32(max_exact)) / np.float32(math.log(REL_MAX_DIST / max_exact))
                         * np.float32(REL_BUCKETS - max_exact)).astype(np.int32)
    large = np.minimum(large, REL_BUCKETS - 1)
    return np.where(dist < max_exact, dist, large)


def _bias_tables(rel_bias):
    assert MOBA_BLOCK + 1 >= REL_MAX_DIST
    kq = np.arange(MOBA_BLOCK)
    d_own = kq[None, :] - kq[:, None]
    buckets = np.stack([_t5_bucket(d_own), _t5_bucket(d_own + MOBA_BLOCK)])
    onehot = jnp.asarray(buckets[None] == np.arange(REL_BUCKETS)[:, None, None, None]).astype(F32)
    table = rel_bias.astype(F32) * LOG2E
    tiles = jnp.einsum('nh,nxkq->hxkq', table, onehot, precision=lax.Precision.HIGHEST)
    cfar = jnp.broadcast_to(table[REL_BUCKETS - 1][:, None, None], (MOBA_HEADS, 1, MOBA_BLOCK))
    return tiles, cfar


def _pack_rows(vectors, width):
    rows = [jnp.pad(v.astype(F32), (0, width - v.shape[0])) for v in vectors]
    rows += [jnp.zeros((width,), F32)] * (8 - len(rows))
    return jnp.stack(rows)


def _pair_spatial(w):
    g, c, _ = w.shape
    return w.reshape(g // 2, 2, c, c).transpose(0, 2, 1, 3).reshape(g // 2, c, 2 * c).astype(BF16)


def kernel(x, mem, norm_g, w_in, gmlp_ln_g, gmlp_ln_b, spatial_w, spatial_b, rel_bias,
           mem_norm_g, w_mem_kv, branch_norm_g, w_out, final_norm_g):
    b, s, d = x.shape
    depth = w_in.shape[0]

    bias, cfar = _bias_tables(rel_bias)
    causal = jnp.tril(jnp.ones((GMLP_CHUNK, GMLP_CHUNK), F32))
    grp = np.arange(D_GMLP) // HEAD_DIM
    avg = jnp.asarray((grp[:, None] == grp[None, :]).astype(np.float32) / HEAD_DIM, BF16)

    w_in_bf, w_out_bf = w_in.astype(BF16), w_out.astype(BF16)
    km, vmt = _mem_kv(mem, mem_norm_g.reshape(depth, 1, d), w_mem_kv.astype(BF16))
    for l in range(depth):
        bg = branch_norm_g[l]
        sb_exp = jnp.repeat(spatial_b[l].T, HEAD_DIM, axis=1)
        vec = _pack_rows([norm_g[l], gmlp_ln_g[l], gmlp_ln_b[l], bg[:D_GMLP], bg[D_GMLP + D_MOBA:],
                          bg[D_GMLP:D_GMLP + D_MOBA], final_norm_g], d)
        x = _layer(x, l, km, vmt, vec, w_in_bf, _pair_spatial(spatial_w[l] * causal[None]),
                   sb_exp, avg, bias, cfar, w_out_bf, final=(l == depth - 1))
    return x
```
